```python
import jax, jax.numpy as jnp
from jax import lax
import numpy as np

D_MODEL = 1024
BATCH = 2
SEQ = 16384
DEPTH = 4
DEC_BATCH = 16
DEC_SEQ = 16
PAST_LEN = 1024

CHUNK = 64
N_MIXERS = 3
N_A_LAYERS = (DEPTH + 2) // 3
N_B_LAYERS = (DEPTH + 1) // 3
N_C_LAYERS = DEPTH // 3
N_DENSE_LAYERS = (DEPTH + 1) // 2
N_MOE_LAYERS = DEPTH // 2
ALPHA = (2 * DEPTH) ** 0.25
BETA = (8 * DEPTH) ** -0.25
LN_EPS = 1e-5
A_CHUNK = 128
A_HALF = D_MODEL
A_GROUPS = 4
HEAD_DIM = 64
B_Q_HEADS = D_MODEL // HEAD_DIM
B_KV_HEADS = 4
B_GROUP = B_Q_HEADS // B_KV_HEADS
WINDOW = 128
BAND_CHUNKS = WINDOW // CHUNK
ROT_DIM = HEAD_DIM // 4
ROPE_THETA = 500000.0
CONV_WIDTH = 31
C_INNER = D_MODEL
D_FF = 2816
N_EXPERTS = 8
TOP_K = 2
D_FF_EXPERT = 3584
MOE_BLOCK = 512

kernel_name = 'hybrid_streaming_gmlp_swa_conformer_step'


def _layer_norm(x, g, b):
    xf = x.astype(jnp.float32)
    mu = jnp.mean(xf, axis=-1, keepdims=True)
    var = jnp.mean(jnp.square(xf - mu), axis=-1, keepdims=True)
    y = (xf - mu) * lax.rsqrt(var + LN_EPS) * g.astype(jnp.float32) + b.astype(jnp.float32)
    return y.astype(x.dtype)


def _rotary(x, pos):
    half = ROT_DIM // 2
    inv = ROPE_THETA ** (-jnp.arange(0, ROT_DIM, 2, dtype=jnp.float32) / ROT_DIM)
    ang = pos.astype(jnp.float32)[:, None] * inv[None, :]
    cos = jnp.cos(ang)[:, None, :]
    sin = jnp.sin(ang)[:, None, :]
    xr = x[..., :ROT_DIM].astype(jnp.float32)
    x1, x2 = xr[..., :half], xr[..., half:]
    rot = jnp.concatenate([x1 * cos - x2 * sin, x2 * cos + x1 * sin], axis=-1)
    return jnp.concatenate([rot.astype(x.dtype), x[..., ROT_DIM:]], axis=-1)


def _gmlp_in(x, w_in, b_in, g, b):
    h = jax.nn.gelu(x @ w_in + b_in)
    u, v = jnp.split(h, 2, axis=-1)
    return u, _layer_norm(v, g, b)


def _gmlp_weights(w_s):
    pos_chunk = jnp.arange(A_CHUNK) // CHUNK
    mask = pos_chunk[None, :] <= pos_chunk[:, None]
    return jnp.where(mask[None], w_s, jnp.zeros_like(w_s))


def _gmlp_prompt(x, w_in, b_in, g, b, w_s, b_s, w_out):
    bsz, t, _ = x.shape
    u, v = _gmlp_in(x, w_in, b_in, g, b)
    vb = v.reshape(bsz, t // A_CHUNK, A_CHUNK, A_GROUPS, A_HALF // A_GROUPS)
    z = jnp.einsum('gij,bnjgc->bnigc', _gmlp_weights(w_s), vb) + b_s.T[:, :, None]
    return (u * z.reshape(bsz, t, A_HALF)) @ w_out


def _gmlp_sample(x, w_in, b_in, g, b, w_s, b_s, w_out):
    bsz, s, _ = x.shape
    u, v = _gmlp_in(x, w_in, b_in, g, b)
    vb = v.reshape(bsz, s, A_GROUPS, A_HALF // A_GROUPS)
    ws = _gmlp_weights(w_s)[:, :s, :s]
    z = jnp.einsum('gij,bjgc->bigc', ws, vb) + b_s.T[:s, :, None]
    return (u * z.reshape(bsz, s, A_HALF)) @ w_out, v


def _qkv(x, w_qkv, b_qkv, pos):
    bsz, t, _ = x.shape
    h = x @ w_qkv + b_qkv
    nq = B_Q_HEADS * HEAD_DIM
    nk = B_KV_HEADS * HEAD_DIM
    q = h[..., :nq].reshape(bsz, t, B_Q_HEADS, HEAD_DIM)
    k = h[..., nq:nq + nk].reshape(bsz, t, B_KV_HEADS, HEAD_DIM)
    v = h[..., nq + nk:].reshape(bsz, t, B_KV_HEADS, HEAD_DIM)
    return _rotary(q, pos), _rotary(k, pos), v


def _sink_attention(q, k, v, valid, sink):
    s = jnp.einsum('bnqhgd,bnkhd->bnhgqk', q, k, preferred_element_type=jnp.float32)
    s = s * (HEAD_DIM ** -0.5)
    s = jnp.where(valid[None, :, None, None], s, -1e30)
    sk = sink.astype(jnp.float32).reshape(B_KV_HEADS, B_GROUP)[None, None, :, :, None, None]
    m = jnp.maximum(jnp.max(s, axis=-1, keepdims=True), sk)
    p = jnp.exp(s - m)
    p = p / (jnp.sum(p, axis=-1, keepdims=True) + jnp.exp(sk - m))
    return jnp.einsum('bnhgqk,bnkhd->bnqhgd', p.astype(v.dtype), v)


def _swa_prompt(x, w_qkv, b_qkv, sink, w_out):
    bsz, t, _ = x.shape
    nb = t // CHUNK
    q, k, v = _qkv(x, w_qkv, b_qkv, jnp.arange(t))
    pad = BAND_CHUNKS * CHUNK

    def band(z):
        zp = jnp.pad(z, ((0, 0), (pad, 0), (0, 0), (0, 0)))
        zp = zp.reshape(bsz, nb + BAND_CHUNKS, CHUNK, B_KV_HEADS, HEAD_DIM)
        return jnp.concatenate([zp[:, j:j + nb] for j in range(BAND_CHUNKS + 1)], axis=2)

    qb = q.reshape(bsz, nb, CHUNK, B_KV_HEADS, B_GROUP, HEAD_DIM)
    key_blk = jnp.arange(nb)[:, None] + jnp.repeat(jnp.arange(BAND_CHUNKS + 1), CHUNK)[None, :] - BAND_CHUNKS
    valid = (key_blk >= 0)[:, None, :]
    o = _sink_attention(qb, band(k), band(v), valid, sink)
    y = o.reshape(bsz, t, B_Q_HEADS * HEAD_DIM) @ w_out
    keep = min(WINDOW, t)
    return y, k[:, t - keep:], v[:, t - keep:]


def _swa_sample(x, ck, cv, w_qkv, b_qkv, sink, w_out):
    bsz, s, _ = x.shape
    nc = ck.shape[1]
    qpos = PAST_LEN + jnp.arange(s)
    q, k, v = _qkv(x, w_qkv, b_qkv, qpos)
    kpos = jnp.concatenate([PAST_LEN - nc + jnp.arange(nc), qpos])
    kk = jnp.concatenate([ck.astype(k.dtype), k], axis=1)
    vv = jnp.concatenate([cv.astype(v.dtype), v], axis=1)
    qc = qpos // CHUNK
    kc = kpos // CHUNK
    valid = ((kc[None, :] <= qc[:, None]) & (kc[None, :] >= qc[:, None] - BAND_CHUNKS))[None]
    qb = q.reshape(bsz, 1, s, B_KV_HEADS, B_GROUP, HEAD_DIM)
    o = _sink_attention(qb, kk[:, None], vv[:, None], valid, sink)
    y = o.reshape(bsz, s, B_Q_HEADS * HEAD_DIM) @ w_out
    return y, k, v


def _glu(x, w_in, b_in):
    a, gt = jnp.split(x @ w_in + b_in, 2, axis=-1)
    return a * jax.nn.sigmoid(gt)


def _conv_tail(hp, w_dw, b_dw, g, b, w_out):
    y = lax.conv_general_dilated(hp, w_dw[:, None, :].astype(hp.dtype), (1,), 'VALID',
                                 dimension_numbers=('NWC', 'WIO', 'NWC'),
                                 feature_group_count=C_INNER) + b_dw
    y = _layer_norm(y, g, b)
    return jax.nn.silu(y) @ w_out


def _conv_prompt(x, w_in, b_in, w_dw, b_dw, g, b, w_out):
    h = _glu(x, w_in, b_in)
    hp = jnp.pad(h, ((0, 0), (CONV_WIDTH - 1, 0), (0, 0)))
    return _conv_tail(hp, w_dw, b_dw, g, b, w_out), hp[:, -(CONV_WIDTH - 1):]


def _conv_sample(x, state, w_in, b_in, w_dw, b_dw, g, b, w_out):
    h = _glu(x, w_in, b_in)
    hp = jnp.concatenate([state.astype(h.dtype), h], axis=1)
    return _conv_tail(hp, w_dw, b_dw, g, b, w_out), hp[:, -(CONV_WIDTH - 1):]


def _swiglu(x, w_in, w_out):
    gt, up = jnp.split(x @ w_in, 2, axis=-1)
    return (jax.nn.silu(gt) * up) @ w_out


def _moe_swiglu(x, w_r, b_r, w_e_in, w_e_out):
    shp = x.shape
    x2 = x.reshape(-1, shp[-1])
    n = x2.shape[0]
    n_assign = n * TOP_K
    logits = jnp.matmul(x2, w_r, preferred_element_type=jnp.float32) + b_r.astype(jnp.float32)
    top_val, top_idx = lax.top_k(logits, TOP_K)
    gates = jax.nn.softmax(top_val, axis=-1)
    flat_e = top_idx.reshape(-1)
    flat_tok = jnp.broadcast_to(jnp.arange(n)[:, None], (n, TOP_K)).reshape(-1)
    order = jnp.argsort(flat_e)
    e_sorted = flat_e[order]
    tok_sorted = flat_tok[order]
    g_sorted = gates.reshape(-1)[order].astype(x.dtype)
    blk = min(MOE_BLOCK, max(8, -(-n_assign // (8 * N_EXPERTS)) * 8))
    counts = jnp.bincount(flat_e, length=N_EXPERTS)
    padded = (counts + blk - 1) // blk * blk
    start = jnp.cumsum(counts) - counts
    pad_end = jnp.cumsum(padded)
    pad_start = pad_end - padded
    dest = pad_start[e_sorted] + jnp.arange(n_assign) - start[e_sorted]
    n_blocks = -(-(n_assign + N_EXPERTS * (blk - 1)) // blk)
    xbuf = jnp.zeros((n_blocks * blk, shp[-1]), x.dtype).at[dest].set(x2[tok_sorted])
    blk_e = jnp.minimum(jnp.searchsorted(pad_end, jnp.arange(n_blocks) * blk, side='right'), N_EXPERTS - 1)

    def expert_block(args):
        xb, e = args
        gt, up = jnp.split(xb @ w_e_in[e], 2, axis=-1)
        return (jax.nn.silu(gt) * up) @ w_e_out[e]

    ybuf = lax.map(expert_block, (xbuf.reshape(n_blocks, blk, shp[-1]), blk_e))
    ybuf = ybuf.reshape(n_blocks * blk, shp[-1])
    y = jax.ops.segment_sum(ybuf[dest] * g_sorted[:, None], tok_sorted, num_segments=n)
    return y.reshape(shp)


def setup_inputs(seed: int = 0) -> dict:
    key = jax.random.key(seed)
    ks = iter(jax.random.split(key, 40))

    def nrm(shape, scale):
        return scale * jax.random.normal(next(ks), shape, dtype=jnp.float32)

    def gain(shape):
        return 1.0 + nrm(shape, 0.05)

    b_cache = min(WINDOW, PAST_LEN)
    qkv_w = (B_Q_HEADS + 2 * B_KV_HEADS) * HEAD_DIM
    return {
        'x_prompt': nrm((BATCH, SEQ, D_MODEL), 1.0),
        'x_sample': nrm((DEC_BATCH, DEC_SEQ, D_MODEL), 1.0),
        'cache_b_k': nrm((N_B_LAYERS, DEC_BATCH, b_cache, B_KV_HEADS, HEAD_DIM), 1.0),
        'cache_b_v': nrm((N_B_LAYERS, DEC_BATCH, b_cache, B_KV_HEADS, HEAD_DIM), 1.0),
        'state_c_conv': nrm((N_C_LAYERS, DEC_BATCH, CONV_WIDTH - 1, C_INNER), 0.5),
        'ln_g': gain((DEPTH, 2, D_MODEL)),
        'ln_b': nrm((DEPTH, 2, D_MODEL), 0.02),
        'w_a_in': nrm((N_A_LAYERS, D_MODEL, 2 * A_HALF), D_MODEL ** -0.5),
        'b_a_in': nrm((N_A_LAYERS, 2 * A_HALF), 0.02),
        'ln_a_g': gain((N_A_LAYERS, A_HALF)),
        'ln_a_b': nrm((N_A_LAYERS, A_HALF), 0.02),
        'w_a_s': nrm((N_A_LAYERS, A_GROUPS, A_CHUNK, A_CHUNK), A_CHUNK ** -0.5),
        'b_a_s': gain((N_A_LAYERS, A_GROUPS, A_CHUNK)),
        'w_a_out': nrm((N_A_LAYERS, A_HALF, D_MODEL), BETA * A_HALF ** -0.5),
        'w_b_qkv': nrm((N_B_LAYERS, D_MODEL, qkv_w), D_MODEL ** -0.5),
        'b_b_qkv': nrm((N_B_LAYERS, qkv_w), 0.02),
        'b_sink': nrm((N_B_LAYERS, B_Q_HEADS), 0.5),
        'w_b_out': nrm((N_B_LAYERS, B_Q_HEADS * HEAD_DIM, D_MODEL), BETA * (B_Q_HEADS * HEAD_DIM) ** -0.5),
        'w_c_in': nrm((N_C_LAYERS, D_MODEL, 2 * C_INNER), D_MODEL ** -0.5),
        'b_c_in': nrm((N_C_LAYERS, 2 * C_INNER), 0.02),
        'w_c_dw': nrm((N_C_LAYERS, CONV_WIDTH, C_INNER), CONV_WIDTH ** -0.5),
        'b_c_dw': nrm((N_C_LAYERS, C_INNER), 0.02),
        'ln_c_g': gain((N_C_LAYERS, C_INNER)),
        'ln_c_b': nrm((N_C_LAYERS, C_INNER), 0.02),
        'w_c_out': nrm((N_C_LAYERS, C_INNER, D_MODEL), BETA * C_INNER ** -0.5),
        'w_f_in': nrm((N_DENSE_LAYERS, D_MODEL, 2 * D_FF), D_MODEL ** -0.5),
        'w_f_out': nrm((N_DENSE_LAYERS, D_FF, D_MODEL), BETA * D_FF ** -0.5),
        'w_r': nrm((N_MOE_LAYERS, D_MODEL, N_EXPERTS), D_MODEL ** -0.5),
        'b_r': nrm((N_MOE_LAYERS, N_EXPERTS), 0.01),
        'w_e_in': nrm((N_MOE_LAYERS, N_EXPERTS, D_MODEL, 2 * D_FF_EXPERT), D_MODEL ** -0.5),
        'w_e_out': nrm((N_MOE_LAYERS, N_EXPERTS, D_FF_EXPERT, D_MODEL), BETA * D_FF_EXPERT ** -0.5),
    }


def reference(x_prompt, x_sample, cache_b_k, cache_b_v, state_c_conv, ln_g, ln_b,
              w_a_in, b_a_in, ln_a_g, ln_a_b, w_a_s, b_a_s, w_a_out,
              w_b_qkv, b_b_qkv, b_sink, w_b_out,
              w_c_in, b_c_in, w_c_dw, b_c_dw, ln_c_g, ln_c_b, w_c_out,
              w_f_in, w_f_out, w_r, b_r, w_e_in, w_e_out):
    xp, xs = x_prompt, x_sample
    a_v_s, b_k_p, b_v_p, b_k_s, b_v_s, c_p, c_s = [], [], [], [], [], [], []
    for i in range(DEPTH):
        kind, j = i % N_MIXERS, i // N_MIXERS
        if kind == 0:
            mp = _gmlp_prompt(xp, w_a_in[j], b_a_in[j], ln_a_g[j], ln_a_b[j], w_a_s[j], b_a_s[j], w_a_out[j])
            ms, v_rows = _gmlp_sample(xs, w_a_in[j], b_a_in[j], ln_a_g[j], ln_a_b[j], w_a_s[j], b_a_s[j], w_a_out[j])
            a_v_s.append(v_rows)
        elif kind == 1:
            mp, kp, vp = _swa_prompt(xp, w_b_qkv[j], b_b_qkv[j], b_sink[j], w_b_out[j])
            ms, kn, vn = _swa_sample(xs, cache_b_k[j], cache_b_v[j], w_b_qkv[j], b_b_qkv[j], b_sink[j], w_b_out[j])
            b_k_p.append(kp)
            b_v_p.append(vp)
            b_k_s.append(kn)
            b_v_s.append(vn)
        else:
            mp, cp = _conv_prompt(xp, w_c_in[j], b_c_in[j], w_c_dw[j], b_c_dw[j], ln_c_g[j], ln_c_b[j], w_c_out[j])
            ms, cs = _conv_sample(xs, state_c_conv[j], w_c_in[j], b_c_in[j], w_c_dw[j], b_c_dw[j], ln_c_g[j], ln_c_b[j], w_c_out[j])
            c_p.append(cp)
            c_s.append(cs)
        xp = _layer_norm(ALPHA * xp + mp, ln_g[i, 0], ln_b[i, 0])
        xs = _layer_norm(ALPHA * xs + ms, ln_g[i, 0], ln_b[i, 0])
        f = i // 2
        if i % 2 == 0:
            fp = _swiglu(xp, w_f_in[f], w_f_out[f])
            fs = _swiglu(xs, w_f_in[f], w_f_out[f])
        else:
            fp = _moe_swiglu(xp, w_r[f], b_r[f], w_e_in[f], w_e_out[f])
            fs = _moe_swiglu(xs, w_r[f], b_r[f], w_e_in[f], w_e_out[f])
        xp = _layer_norm(ALPHA * xp + fp, ln_g[i, 1], ln_b[i, 1])
        xs = _layer_norm(ALPHA * xs + fs, ln_g[i, 1], ln_b[i, 1])
    return (xp, xs, jnp.stack(a_v_s), jnp.stack(b_k_p), jnp.stack(b_v_p), jnp.stack(b_k_s), jnp.stack(b_v_s), jnp.stack(c_p), jnp.stack(c_s))
```

```python
import functools

import jax
import jax.numpy as jnp
import numpy as np
from jax import lax
from jax.experimental import pallas as pl
from jax.experimental.pallas import tpu as pltpu

D_MODEL = 1024
DEPTH = 4
PAST_LEN = 1024
CHUNK = 64
N_MIXERS = 3
ALPHA = (2 * DEPTH) ** 0.25
LN_EPS = 1e-5
A_CHUNK = 128
A_GROUPS = 4
HEAD_DIM = 64
B_Q_HEADS = D_MODEL // HEAD_DIM
B_KV_HEADS = 4
B_GROUP = B_Q_HEADS // B_KV_HEADS
WINDOW = 128
BAND_CHUNKS = WINDOW // CHUNK
ROT_DIM = HEAD_DIM // 4
ROPE_THETA = 500000.0
CONV_WIDTH = 31
N_EXPERTS = 8
TOP_K = 2
MOE_BLOCK = 512

VMEM_LIMIT_BYTES = 56 * 1024 * 1024
ROW_TILE = 512
FF_SUB = 256
CONV_HALO = 32
GROUP_LANES = B_GROUP * HEAD_DIM

_BF16 = jnp.bfloat16
_F32 = jnp.float32


def _cparams(*sem):
    return pltpu.CompilerParams(dimension_semantics=sem, vmem_limit_bytes=VMEM_LIMIT_BYTES)


def _dot(a, b):
    return jnp.dot(a, b, preferred_element_type=_F32)


def _layer_norm(x, g, b):
    mu = jnp.mean(x, axis=-1, keepdims=True)
    xc = x - mu
    var = jnp.mean(xc * xc, axis=-1, keepdims=True)
    return xc * lax.rsqrt(var + LN_EPS) * g + b


def _sigmoid(x):
    return 1.0 / (1.0 + jnp.exp(-x))


def _full(shape):
    nd = len(shape)
    return pl.BlockSpec(shape, lambda *_: (0,) * nd)


def _gmlp_kernel(x_ref, w_in_ref, b_in_ref, g_a_ref, b_a_ref, w_s_ref, bz_ref, w_out_ref,
                 g_ref, b_ref, *out_refs, chunk, emit_v):
    o_ref = out_refs[0]
    x = x_ref[...]
    tm = x.shape[0]
    h = _dot(x.astype(_BF16), w_in_ref[...]) + b_in_ref[...]
    h = jax.nn.gelu(h)
    u = h[:, :D_MODEL]
    v = _layer_norm(h[:, D_MODEL:], g_a_ref[...], b_a_ref[...])
    if emit_v:
        out_refs[1][...] = v
    vb = v.astype(_BF16)
    ri = lax.broadcasted_iota(jnp.int32, (chunk, chunk), 0) // CHUNK
    ci = lax.broadcasted_iota(jnp.int32, (chunk, chunk), 1) // CHUNK
    gw = D_MODEL // A_GROUPS
    ws = [jnp.where(ci <= ri, w_s_ref[g], 0.0).astype(_BF16) for g in range(A_GROUPS)]
    rows = []
    for n in range(tm // chunk):
        r0 = n * chunk
        cols = [_dot(ws[g], vb[r0:r0 + chunk, g * gw:(g + 1) * gw]) for g in range(A_GROUPS)]
        rows.append(jnp.concatenate(cols, axis=1) + bz_ref[...])
    z = jnp.concatenate(rows, axis=0) if len(rows) > 1 else rows[0]
    y = _dot((u * z).astype(_BF16), w_out_ref[...])
    o_ref[...] = _layer_norm(ALPHA * x + y, g_ref[...], b_ref[...])


def _gmlp(x2, w_in, b_in, g_a, b_a, w_s, b_s, w_out, g, b, *, chunk, tm, emit_v):
    n = x2.shape[0]
    d = D_MODEL
    gw = d // A_GROUPS
    bz = jnp.repeat(b_s[:, :chunk].T, gw, axis=1)
    out_shape = [jax.ShapeDtypeStruct((n, d), _F32)]
    out_specs = [pl.BlockSpec((tm, d), lambda i: (i, 0))]
    if emit_v:
        out_shape.append(jax.ShapeDtypeStruct((n, d), _F32))
        out_specs.append(pl.BlockSpec((tm, d), lambda i: (i, 0)))
    res = pl.pallas_call(
        functools.partial(_gmlp_kernel, chunk=chunk, emit_v=emit_v),
        grid=(n // tm,),
        in_specs=[
            pl.BlockSpec((tm, d), lambda i: (i, 0)),
            _full((d, 2 * d)), _full((1, 2 * d)), _full((1, d)), _full((1, d)),
            _full((A_GROUPS, chunk, chunk)), _full((chunk, d)), _full((d, d)),
            _full((1, d)), _full((1, d)),
        ],
        out_specs=out_specs,
        out_shape=out_shape,
        compiler_params=_cparams("arbitrary"),
        name="gmlp_mixer",
    )(x2, w_in.astype(_BF16), b_in[None], g_a[None], b_a[None], w_s[:, :chunk, :chunk], bz,
      w_out.astype(_BF16), g[None], b[None])
    return res


def _rope_tables(pos):
    half = ROT_DIM // 2
    inv = ROPE_THETA ** (-np.arange(0, ROT_DIM, 2, dtype=np.float32) / ROT_DIM)
    ang = pos.astype(_F32)[:, None] * jnp.asarray(inv, _F32)[None, :]
    cos, sin = jnp.cos(ang), jnp.sin(ang)
    zero = jnp.zeros((pos.shape[0], HEAD_DIM - ROT_DIM), _F32)
    zh = jnp.zeros_like(sin)
    c = jnp.concatenate([cos, cos, zero + 1.0], axis=1)
    s_up = jnp.concatenate([zh, sin, zero], axis=1)
    s_dn = jnp.concatenate([-sin, zh, zero], axis=1)
    rep = lambda t: jnp.concatenate([t, t], axis=1)
    return rep(c), rep(s_up), rep(s_dn)


def _rotary(x, c, s_up, s_dn):
    w = x.shape[1]
    half = ROT_DIM // 2
    reps = w // c.shape[1]
    t = lambda a: jnp.concatenate([a] * reps, axis=1)
    return (x * t(c) + pltpu.roll(x, half, axis=1) * t(s_up)
            + pltpu.roll(x, w - half, axis=1) * t(s_dn))


def _expand_kv_weights(w_qkv, b_qkv):
    nq = B_Q_HEADS * HEAD_DIM
    nk = B_KV_HEADS * HEAD_DIM

    def rep(a):
        lead = a.shape[:-1]
        a = a.reshape(lead + (B_KV_HEADS, 1, HEAD_DIM))
        a = jnp.broadcast_to(a, lead + (B_KV_HEADS, B_GROUP, HEAD_DIM))
        return a.reshape(lead + (nq,))

    w = jnp.concatenate([w_qkv[:, :nq], rep(w_qkv[:, nq:nq + nk]), rep(w_qkv[:, nq + nk:])], axis=1)
    bb = jnp.concatenate([b_qkv[:nq], rep(b_qkv[nq:nq + nk]), rep(b_qkv[nq + nk:])])
    return w, bb


def _attend(q, kwin, vwin, valid, sink_col):
    nq = q.shape[0]
    lane_head = lax.broadcasted_iota(jnp.int32, (1, GROUP_LANES), 1) // HEAD_DIM
    masks = [(lane_head == g).astype(_F32) for g in range(B_GROUP)]
    qs = jnp.concatenate([q * masks[g] for g in range(B_GROUP)], axis=0).astype(_BF16)
    s = lax.dot_general(qs, kwin, (((1,), (1,)), ((), ())), preferred_element_type=_F32)
    s = s * (HEAD_DIM ** -0.5)
    s = jnp.where(valid, s, -1e30)
    m = jnp.maximum(jnp.max(s, axis=-1, keepdims=True), sink_col)
    p = jnp.exp(s - m)
    p = p / (jnp.sum(p, axis=-1, keepdims=True) + jnp.exp(sink_col - m))
    o = _dot(p.astype(_BF16), vwin)
    out = o[0:nq] * masks[0]
    for g in range(1, B_GROUP):
        out = out + o[g * nq:(g + 1) * nq] * masks[g]
    return out


def _sink_cols(sink_ref, hk, nq):
    return jnp.concatenate(
        [jnp.full((nq, 1), sink_ref[hk * B_GROUP + g], _F32) for g in range(B_GROUP)], axis=0)


def _swa_prompt_kernel(sink_ref, x_ref, w_ref, bq_ref, c_ref, su_ref, sd_ref, w_out_ref, g_ref, b_ref,
                       o_ref, kc_ref, vc_ref, k_ext, v_ext):
    t = pl.program_id(1)
    tm = x_ref.shape[0]
    d = D_MODEL
    halo = BAND_CHUNKS * CHUNK

    @pl.when(t == 0)
    def _():
        k_ext[0:halo, :] = jnp.zeros((halo, d), _BF16)
        v_ext[0:halo, :] = jnp.zeros((halo, d), _BF16)

    x = x_ref[...]
    qkv = _dot(x.astype(_BF16), w_ref[...]) + bq_ref[...]
    qk = _rotary(qkv[:, :2 * d], c_ref[...], su_ref[...], sd_ref[...])
    q = qk[:, :d]
    k = qk[:, d:]
    v = qkv[:, 2 * d:]
    k_ext[halo:halo + tm, :] = k.astype(_BF16)
    v_ext[halo:halo + tm, :] = v.astype(_BF16)
    kc_ref[...] = k[tm - WINDOW:, :]
    vc_ref[...] = v[tm - WINDOW:, :]

    nkeys = (BAND_CHUNKS + 1) * CHUNK
    key_chunk = lax.broadcasted_iota(jnp.int32, (1, nkeys), 1) // CHUNK
    rows = []
    for n in range(tm // CHUNK):
        valid = (t * (tm // CHUNK) + n - BAND_CHUNKS + key_chunk) >= 0
        r0 = n * CHUNK
        cols = []
        for hk in range(B_KV_HEADS):
            l0 = hk * GROUP_LANES
            cols.append(_attend(q[r0:r0 + CHUNK, l0:l0 + GROUP_LANES],
                                k_ext[r0:r0 + nkeys, l0:l0 + GROUP_LANES],
                                v_ext[r0:r0 + nkeys, l0:l0 + GROUP_LANES],
                                valid, _sink_cols(sink_ref, hk, CHUNK)))
        rows.append(jnp.concatenate(cols, axis=1))
    att = jnp.concatenate(rows, axis=0)
    y = _dot(att.astype(_BF16), w_out_ref[...])
    o_ref[...] = _layer_norm(ALPHA * x + y, g_ref[...], b_ref[...])
    k_ext[0:halo, :] = k_ext[tm:tm + halo, :]
    v_ext[0:halo, :] = v_ext[tm:tm + halo, :]


def _unexpand(a):
    lead = a.shape[:-1]
    return a.reshape(lead + (B_KV_HEADS, B_GROUP, HEAD_DIM))[..., 0, :]


def _swa_prompt(x3, w_qkv, b_qkv, sink, w_out, g, b, *, tm):
    bsz, t, d = x3.shape
    w_exp, b_exp = _expand_kv_weights(w_qkv, b_qkv)
    c, su, sd = _rope_tables(jnp.arange(t))
    tab = pl.BlockSpec((tm, 2 * HEAD_DIM), lambda bi, ti, *_: (ti, 0))
    xspec = pl.BlockSpec((None, tm, d), lambda bi, ti, *_: (bi, ti, 0))
    cspec = pl.BlockSpec((None, WINDOW, d), lambda bi, ti, *_: (bi, 0, 0))
    halo = BAND_CHUNKS * CHUNK
    y, kc, vc = pl.pallas_call(
        _swa_prompt_kernel,
        grid_spec=pltpu.PrefetchScalarGridSpec(
            num_scalar_prefetch=1,
            grid=(bsz, t // tm),
            in_specs=[xspec, _full((d, 3 * d)), _full((1, 3 * d)), tab, tab, tab, _full((d, d)),
                      _full((1, d)), _full((1, d))],
            out_specs=[xspec, cspec, cspec],
            scratch_shapes=[pltpu.VMEM((tm + halo, d), _BF16), pltpu.VMEM((tm + halo, d), _BF16)],
        ),
        out_shape=[jax.ShapeDtypeStruct((bsz, t, d), _F32),
                   jax.ShapeDtypeStruct((bsz, WINDOW, d), _F32),
                   jax.ShapeDtypeStruct((bsz, WINDOW, d), _F32)],
        compiler_params=_cparams("arbitrary", "arbitrary"),
        name="swa_prompt_mixer",
    )(sink, x3, w_exp.astype(_BF16), b_exp[None], c, su, sd, w_out.astype(_BF16), g[None], b[None])
    return y, _unexpand(kc), _unexpand(vc)


def _swa_sample_kernel(sink_ref, x_ref, ck_ref, cv_ref, valid_ref, w_ref, bq_ref, c_ref, su_ref, sd_ref,
                       w_out_ref, g_ref, b_ref, o_ref, kn_ref, vn_ref, *, bsz, s):
    d = D_MODEL
    x = x_ref[...]
    qkv = _dot(x.astype(_BF16), w_ref[...]) + bq_ref[...]
    qk = _rotary(qkv[:, :2 * d], c_ref[...], su_ref[...], sd_ref[...])
    q = qk[:, :d]
    k = qk[:, d:]
    v = qkv[:, 2 * d:]
    kn_ref[...] = k
    vn_ref[...] = v
    kb = k.astype(_BF16)
    vb = v.astype(_BF16)
    valid = jnp.concatenate([valid_ref[...] > 0] * B_GROUP, axis=0)
    rows = []
    for bi in range(bsz):
        r0 = bi * s
        kk = jnp.concatenate([ck_ref[bi], kb[r0:r0 + s]], axis=0)
        vv = jnp.concatenate([cv_ref[bi], vb[r0:r0 + s]], axis=0)
        cols = []
        for hk in range(B_KV_HEADS):
            l0 = hk * GROUP_LANES
            cols.append(_attend(q[r0:r0 + s, l0:l0 + GROUP_LANES], kk[:, l0:l0 + GROUP_LANES],
                                vv[:, l0:l0 + GROUP_LANES], valid, _sink_cols(sink_ref, hk, s)))
        rows.append(jnp.concatenate(cols, axis=1))
    att = jnp.concatenate(rows, axis=0)
    y = _dot(att.astype(_BF16), w_out_ref[...])
    o_ref[...] = _layer_norm(ALPHA * x + y, g_ref[...], b_ref[...])


def _swa_sample(x3, ck, cv, w_qkv, b_qkv, sink, w_out, g, b):
    bsz, s, d = x3.shape
    nc = ck.shape[1]
    w_exp, b_exp = _expand_kv_weights(w_qkv, b_qkv)
    qpos = PAST_LEN + np.arange(s)
    kpos = np.concatenate([PAST_LEN - nc + np.arange(nc), qpos])
    qc, kc = qpos // CHUNK, kpos // CHUNK
    valid = ((kc[None, :] <= qc[:, None]) & (kc[None, :] >= qc[:, None] - BAND_CHUNKS)).astype(np.int32)
    c, su, sd = _rope_tables(jnp.tile(jnp.asarray(qpos), bsz))

    def expand_cache(a):
        a = jnp.broadcast_to(a[:, :, :, None, :], (bsz, nc, B_KV_HEADS, B_GROUP, HEAD_DIM))
        return a.reshape(bsz, nc, d).astype(_BF16)

    n = bsz * s
    vm = pl.BlockSpec(memory_space=pltpu.VMEM)
    y, kn, vn = pl.pallas_call(
        functools.partial(_swa_sample_kernel, bsz=bsz, s=s),
        in_specs=[pl.BlockSpec(memory_space=pltpu.SMEM)] + [vm] * 12,
        out_specs=[vm, vm, vm],
        out_shape=[jax.ShapeDtypeStruct((n, d), _F32)] * 3,
        compiler_params=pltpu.CompilerParams(vmem_limit_bytes=VMEM_LIMIT_BYTES),
        name="swa_sample_mixer",
    )(sink, x3.reshape(n, d), expand_cache(ck), expand_cache(cv), jnp.asarray(valid),
      w_exp.astype(_BF16), b_exp[None], c, su, sd, w_out.astype(_BF16), g[None], b[None])
    return (y.reshape(bsz, s, d), _unexpand(kn.reshape(bsz, s, d)), _unexpand(vn.reshape(bsz, s, d)))


def _conv_kernel(x_ref, st_ref, w_in_ref, b_in_ref, w_dw_ref, b_dw_ref, g_c_ref, b_c_ref, w_out_ref,
                 g_ref, b_ref, o_ref, st_out_ref, h_ext):
    t = pl.program_id(1)
    tm = x_ref.shape[0]
    d = D_MODEL

    @pl.when(t == 0)
    def _():
        h_ext[0:CONV_HALO, :] = st_ref[...]

    x = x_ref[...]
    ag = _dot(x.astype(_BF16), w_in_ref[...]) + b_in_ref[...]
    h = ag[:, :d] * _sigmoid(ag[:, d:])
    h_ext[CONV_HALO:CONV_HALO + tm, :] = h
    base = CONV_HALO - (CONV_WIDTH - 1)
    acc = jnp.zeros((tm, d), _F32)
    for sub in range(8):
        offs = [o for o in range(base, base + CONV_WIDTH) if o % 8 == sub]
        if not offs:
            continue
        span = max(offs) - sub + tm
        sh = h_ext[sub:sub + span, :]
        for o in offs:
            a0 = o - sub
            acc = acc + sh[a0:a0 + tm, :] * w_dw_ref[o - base:o - base + 1, :]
    y = acc + b_dw_ref[...]
    y = _layer_norm(y, g_c_ref[...], b_c_ref[...])
    y = y * _sigmoid(y)
    y = _dot(y.astype(_BF16), w_out_ref[...])
    o_ref[...] = _layer_norm(ALPHA * x + y, g_ref[...], b_ref[...])
    new_hist = h_ext[tm:tm + CONV_HALO, :]
    st_out_ref[...] = new_hist
    h_ext[0:CONV_HALO, :] = new_hist


def _conv(x3, state, w_in, b_in, w_dw, b_dw, g_c, b_c, w_out, g, b, *, tm):
    bsz, t, d = x3.shape
    st = jnp.pad(state, ((0, 0), (CONV_HALO - (CONV_WIDTH - 1), 0), (0, 0)))
    xspec = pl.BlockSpec((None, tm, d), lambda bi, ti: (bi, ti, 0))
    sspec = pl.BlockSpec((None, CONV_HALO, d), lambda bi, ti: (bi, 0, 0))
    y, st_new = pl.pallas_call(
        _conv_kernel,
        grid=(bsz, t // tm),
        in_specs=[xspec, sspec, _full((d, 2 * d)), _full((1, 2 * d)), _full((CONV_WIDTH, d)), _full((1, d)),
                  _full((1, d)), _full((1, d)), _full((d, d)), _full((1, d)), _full((1, d))],
        out_specs=[xspec, sspec],
        out_shape=[jax.ShapeDtypeStruct((bsz, t, d), _F32),
                   jax.ShapeDtypeStruct((bsz, CONV_HALO, d), _F32)],
        scratch_shapes=[pltpu.VMEM((tm + CONV_HALO, d), _F32)],
        compiler_params=_cparams("arbitrary", "arbitrary"),
        name="conv_mixer",
    )(x3, st, w_in.astype(_BF16), b_in[None], w_dw, b_dw[None], g_c[None], b_c[None],
      w_out.astype(_BF16), g[None], b[None])
    return y, st_new[:, CONV_HALO - (CONV_WIDTH - 1):]


def _swiglu_partial(xb, wg_ref, wu_ref, wo_ref, width):
    acc = None
    for c in range(width // FF_SUB):
        sl = slice(c * FF_SUB, (c + 1) * FF_SUB)
        gt = _dot(xb, wg_ref[:, sl])
        up = _dot(xb, wu_ref[:, sl])
        hh = (gt * _sigmoid(gt) * up).astype(_BF16)
        part = _dot(hh, wo_ref[sl, :])
        acc = part if acc is None else acc + part
    return acc


def _ffn_kernel(x_ref, wg_ref, wu_ref, wo_ref, g_ref, b_ref, o_ref, *, d_ff):
    x = x_ref[...]
    f = _swiglu_partial(x.astype(_BF16), wg_ref, wu_ref, wo_ref, d_ff)
    o_ref[...] = _layer_norm(ALPHA * x + f, g_ref[...], b_ref[...])


def _ffn(x2, w_in, w_out, g, b, *, tm):
    n, d = x2.shape
    d_ff = w_out.shape[0]
    w_in_b = w_in.astype(_BF16)
    one = pl.Buffered(1)
    return pl.pallas_call(
        functools.partial(_ffn_kernel, d_ff=d_ff),
        grid=(n // tm,),
        in_specs=[
            pl.BlockSpec((tm, d), lambda i: (i, 0)),
            pl.BlockSpec((d, d_ff), lambda i: (0, 0), pipeline_mode=one),
            pl.BlockSpec((d, d_ff), lambda i: (0, 1), pipeline_mode=one),
            pl.BlockSpec((d_ff, d), lambda i: (0, 0), pipeline_mode=one),
            _full((1, d)), _full((1, d)),
        ],
        out_specs=pl.BlockSpec((tm, d), lambda i: (i, 0)),
        out_shape=jax.ShapeDtypeStruct((n, d), _F32),
        compiler_params=_cparams("arbitrary"),
        name="dense_swiglu",
    )(x2, w_in_b, w_in_b, w_out.astype(_BF16), g[None], b[None])


def _moe_kernel(blk_e_ref, nblk_ref, x_ref, wg_ref, wu_ref, wo_ref, o_ref, acc_ref, *, width):
    bi = pl.program_id(0)
    j = pl.program_id(1)
    live = bi < nblk_ref[0]

    @pl.when(live)
    def _():
        part = _swiglu_partial(x_ref[...].astype(_BF16), wg_ref, wu_ref, wo_ref, width)

        @pl.when(j == 0)
        def _():
            acc_ref[...] = part

        @pl.when(j > 0)
        def _():
            acc_ref[...] += part

    @pl.when(j == pl.num_programs(1) - 1)
    def _():
        o_ref[...] = jnp.where(live, acc_ref[...], 0.0)


def _moe_experts(xbuf, blk_e, nblk, w_e_in, w_e_out, *, blk):
    rows, d = xbuf.shape
    n_blocks = rows // blk
    d_ff = w_e_out.shape[1]
    splits = 2
    width = d_ff // splits
    return pl.pallas_call(
        functools.partial(_moe_kernel, width=width),
        grid_spec=pltpu.PrefetchScalarGridSpec(
            num_scalar_prefetch=2,
            grid=(n_blocks, splits),
            in_specs=[
                pl.BlockSpec((blk, d), lambda i, j, e, nb: (i, 0)),
                pl.BlockSpec((None, d, width), lambda i, j, e, nb: (e[i], 0, j)),
                pl.BlockSpec((None, d, width), lambda i, j, e, nb: (e[i], 0, splits + j)),
                pl.BlockSpec((None, width, d), lambda i, j, e, nb: (e[i], j, 0)),
            ],
            out_specs=pl.BlockSpec((blk, d), lambda i, j, e, nb: (i, 0)),
            scratch_shapes=[pltpu.VMEM((blk, d), _F32)],
        ),
        out_shape=jax.ShapeDtypeStruct((rows, d), _F32),
        compiler_params=_cparams("arbitrary", "arbitrary"),
        name="moe_experts",
    )(blk_e, nblk, xbuf, w_e_in, w_e_in, w_e_out)


def _res_ln_kernel(x_ref, f_ref, g_ref, b_ref, o_ref):
    o_ref[...] = _layer_norm(ALPHA * x_ref[...] + f_ref[...], g_ref[...], b_ref[...])


def _res_ln(x2, f2, g, b, *, tm):
    n, d = x2.shape
    row = pl.BlockSpec((tm, d), lambda i: (i, 0))
    return pl.pallas_call(
        _res_ln_kernel,
        grid=(n // tm,),
        in_specs=[row, row, _full((1, d)), _full((1, d))],
        out_specs=row,
        out_shape=jax.ShapeDtypeStruct((n, d), _F32),
        compiler_params=_cparams("arbitrary"),
        name="residual_layernorm",
    )(x2, f2, g[None], b[None])


def _moe(x2, w_r, b_r, w_e_in, w_e_out, g, b, *, tm):
    n, d = x2.shape
    n_assign = n * TOP_K
    logits = jnp.matmul(x2, w_r, preferred_element_type=_F32, precision=lax.Precision.HIGHEST) + b_r
    top_val, top_idx = lax.top_k(logits, TOP_K)
    gates = jax.nn.softmax(top_val, axis=-1)
    flat_e = top_idx.reshape(-1)
    blk = min(MOE_BLOCK, max(8, -(-n_assign // (8 * N_EXPERTS)) * 8))
    onehot = (flat_e[:, None] == jnp.arange(N_EXPERTS)[None, :]).astype(jnp.int32)
    csum = jnp.cumsum(onehot, axis=0)
    counts = csum[-1]
    rank = jnp.take_along_axis(csum, flat_e[:, None], axis=1)[:, 0] - 1
    padded = (counts + blk - 1) // blk * blk
    pad_end = jnp.cumsum(padded)
    pad_start = pad_end - padded
    dest = pad_start[flat_e] + rank
    n_blocks = -(-(n_assign + N_EXPERTS * (blk - 1)) // blk)
    blk_e = jnp.minimum(jnp.searchsorted(pad_end, jnp.arange(n_blocks) * blk, side='right'),
                        N_EXPERTS - 1).astype(jnp.int32)
    nblk = (pad_end[-1] // blk).astype(jnp.int32)[None]
    src = jnp.full((n_blocks * blk,), n, jnp.int32).at[dest].set(jnp.arange(n_assign, dtype=jnp.int32) // TOP_K)
    xpad = jnp.concatenate([x2, jnp.zeros((1, d), x2.dtype)], axis=0)
    xbuf = xpad[src]
    ybuf = _moe_experts(xbuf, blk_e, nblk, w_e_in.astype(_BF16), w_e_out.astype(_BF16), blk=blk)
    dest2 = dest.reshape(n, TOP_K)
    f = ybuf[dest2[:, 0]] * gates[:, 0:1] + ybuf[dest2[:, 1]] * gates[:, 1:2]
    return _res_ln(x2, f, g, b, tm=tm)


def kernel(x_prompt, x_sample, cache_b_k, cache_b_v, state_c_conv, ln_g, ln_b, w_a_in, b_a_in, ln_a_g, ln_a_b,
           w_a_s, b_a_s, w_a_out, w_b_qkv, b_b_qkv, b_sink, w_b_out, w_c_in, b_c_in, w_c_dw, b_c_dw, ln_c_g,
           ln_c_b, w_c_out, w_f_in, w_f_out, w_r, b_r, w_e_in, w_e_out):
    bsz, t, d = x_prompt.shape
    dbs, s, _ = x_sample.shape
    n_p, n_s = bsz * t, dbs * s
    xp = x_prompt.reshape(n_p, d)
    xs = x_sample.reshape(n_s, d)
    a_v_s, b_k_p, b_v_p, b_k_s, b_v_s, c_p, c_s = [], [], [], [], [], [], []
    for i in range(DEPTH):
        kind, j = i % N_MIXERS, i // N_MIXERS
        g0, b0 = ln_g[i, 0], ln_b[i, 0]
        if kind == 0:
            args = (w_a_in[j], b_a_in[j], ln_a_g[j], ln_a_b[j], w_a_s[j], b_a_s[j], w_a_out[j], g0, b0)
            (xp,) = _gmlp(xp, *args, chunk=A_CHUNK, tm=ROW_TILE, emit_v=False)
            xs, v_rows = _gmlp(xs, *args, chunk=s, tm=n_s, emit_v=True)
            a_v_s.append(v_rows.reshape(dbs, s, d))
        elif kind == 1:
            args = (w_b_qkv[j], b_b_qkv[j], b_sink[j], w_b_out[j], g0, b0)
            xp3, kp, vp = _swa_prompt(xp.reshape(bsz, t, d), *args, tm=ROW_TILE)
            xs3, kn, vn = _swa_sample(xs.reshape(dbs, s, d), cache_b_k[j], cache_b_v[j], *args)
            xp, xs = xp3.reshape(n_p, d), xs3.reshape(n_s, d)
            b_k_p.append(kp)
            b_v_p.append(vp)
            b_k_s.append(kn)
            b_v_s.append(vn)
        else:
            args = (w_c_in[j], b_c_in[j], w_c_dw[j], b_c_dw[j], ln_c_g[j], ln_c_b[j], w_c_out[j], g0, b0)
            zero_state = jnp.zeros((bsz, CONV_WIDTH - 1, d), _F32)
            xp3, cp = _conv(xp.reshape(bsz, t, d), zero_state, *args, tm=ROW_TILE)
            xs3, cs = _conv(xs.reshape(dbs, s, d), state_c_conv[j], *args, tm=s)
            xp, xs = xp3.reshape(n_p, d), xs3.reshape(n_s, d)
            c_p.append(cp)
            c_s.append(cs)
        f = i // 2
        g1, b1 = ln_g[i, 1], ln_b[i, 1]
        if i % 2 == 0:
            xp = _ffn(xp, w_f_in[f], w_f_out[f], g1, b1, tm=ROW_TILE)
            xs = _ffn(xs, w_f_in[f], w_f_out[f], g1, b1, tm=n_s)
        else:
            xp = _moe(xp, w_r[f], b_r[f], w_e_in[f], w_e_out[f], g1, b1, tm=ROW_TILE)
            xs = _moe(xs, w_r[f], b_r[f], w_e_in[f], w_e_out[f], g1, b1, tm=n_s)
    return (xp.reshape(bsz, t, d), xs.reshape(dbs, s, d), jnp.stack(a_v_s), jnp.stack(b_k_p),
            jnp.stack(b_v_p), jnp.stack(b_k_s), jnp.stack(b_v_s), jnp.stack(c_p), jnp.stack(c_s))
```

```python
import functools

import jax
import jax.numpy as jnp
import numpy as np
from jax import lax
from jax.experimental import pallas as pl
from jax.experimental.pallas import tpu as pltpu

D_MODEL = 1024
DEPTH = 4
PAST_LEN = 1024
CHUNK = 64
N_MIXERS = 3
ALPHA = (2 * DEPTH) ** 0.25
LN_EPS = 1e-5
A_CHUNK = 128
A_GROUPS = 4
HEAD_DIM = 64
B_Q_HEADS = D_MODEL // HEAD_DIM
B_KV_HEADS = 4
B_GROUP = B_Q_HEADS // B_KV_HEADS
WINDOW = 128
BAND_CHUNKS = WINDOW // CHUNK
ROT_DIM = HEAD_DIM // 4
ROPE_THETA = 500000.0
CONV_WIDTH = 31
N_EXPERTS = 8
TOP_K = 2
MOE_BLOCK = 512

VMEM_LIMIT_BYTES = 56 * 1024 * 1024
ROW_TILE = 512
FF_SUB = 256
CONV_HALO = 32
GROUP_LANES = B_GROUP * HEAD_DIM

_BF16 = jnp.bfloat16
_F32 = jnp.float32


def _cparams(*sem):
    return pltpu.CompilerParams(dimension_semantics=sem, vmem_limit_bytes=VMEM_LIMIT_BYTES)


def _dot(a, b):
    return jnp.dot(a, b, preferred_element_type=_F32)


def _layer_norm(x, g, b):
    mu = jnp.mean(x, axis=-1, keepdims=True)
    xc = x - mu
    var = jnp.mean(xc * xc, axis=-1, keepdims=True)
    return xc * lax.rsqrt(var + LN_EPS) * g + b


def _sigmoid(x):
    return 1.0 / (1.0 + jnp.exp(-x))


def _full(shape):
    nd = len(shape)
    return pl.BlockSpec(shape, lambda *_: (0,) * nd)


def _gmlp_kernel(x_ref, w_in_ref, b_in_ref, g_a_ref, b_a_ref, w_s_ref, bz_ref, w_out_ref,
                 g_ref, b_ref, *out_refs, chunk, emit_v):
    o_ref = out_refs[0]
    x = x_ref[...]
    tm = x.shape[0]
    h = _dot(x.astype(_BF16), w_in_ref[...]) + b_in_ref[...]
    h = jax.nn.gelu(h)
    u = h[:, :D_MODEL]
    v = _layer_norm(h[:, D_MODEL:], g_a_ref[...], b_a_ref[...])
    if emit_v:
        out_refs[1][...] = v
    vb = v.astype(_BF16)
    ri = lax.broadcasted_iota(jnp.int32, (chunk, chunk), 0) // CHUNK
    ci = lax.broadcasted_iota(jnp.int32, (chunk, chunk), 1) // CHUNK
    gw = D_MODEL // A_GROUPS
    ws = [jnp.where(ci <= ri, w_s_ref[g], 0.0).astype(_BF16) for g in range(A_GROUPS)]
    rows = []
    for n in range(tm // chunk):
        r0 = n * chunk
        cols = [_dot(ws[g], vb[r0:r0 + chunk, g * gw:(g + 1) * gw]) for g in range(A_GROUPS)]
        rows.append(jnp.concatenate(cols, axis=1) + bz_ref[...])
    z = jnp.concatenate(rows, axis=0) if len(rows) > 1 else rows[0]
    y = _dot((u * z).astype(_BF16), w_out_ref[...])
    o_ref[...] = _layer_norm(ALPHA * x + y, g_ref[...], b_ref[...])


def _gmlp(x2, w_in, b_in, g_a, b_a, w_s, b_s, w_out, g, b, *, chunk, tm, emit_v):
    n = x2.shape[0]
    d = D_MODEL
    gw = d // A_GROUPS
    bz = jnp.repeat(b_s[:, :chunk].T, gw, axis=1)
    out_shape = [jax.ShapeDtypeStruct((n, d), _F32)]
    out_specs = [pl.BlockSpec((tm, d), lambda i: (i, 0))]
    if emit_v:
        out_shape.append(jax.ShapeDtypeStruct((n, d), _F32))
        out_specs.append(pl.BlockSpec((tm, d), lambda i: (i, 0)))
    res = pl.pallas_call(
        functools.partial(_gmlp_kernel, chunk=chunk, emit_v=emit_v),
        grid=(n // tm,),
        in_specs=[
            pl.BlockSpec((tm, d), lambda i: (i, 0)),
            _full((d, 2 * d)), _full((1, 2 * d)), _full((1, d)), _full((1, d)),
            _full((A_GROUPS, chunk, chunk)), _full((chunk, d)), _full((d, d)),
            _full((1, d)), _full((1, d)),
        ],
        out_specs=out_specs,
        out_shape=out_shape,
        compiler_params=_cparams("arbitrary"),
        name="gmlp_mixer",
    )(x2, w_in.astype(_BF16), b_in[None], g_a[None], b_a[None], w_s[:, :chunk, :chunk], bz,
      w_out.astype(_BF16), g[None], b[None])
    return res


def _rope_tables(pos):
    half = ROT_DIM // 2
    inv = ROPE_THETA ** (-np.arange(0, ROT_DIM, 2, dtype=np.float32) / ROT_DIM)
    ang = pos.astype(_F32)[:, None] * jnp.asarray(inv, _F32)[None, :]
    cos, sin = jnp.cos(ang), jnp.sin(ang)
    zero = jnp.zeros((pos.shape[0], HEAD_DIM - ROT_DIM), _F32)
    zh = jnp.zeros_like(sin)
    c = jnp.concatenate([cos, cos, zero + 1.0], axis=1)
    s_up = jnp.concatenate([zh, sin, zero], axis=1)
    s_dn = jnp.concatenate([-sin, zh, zero], axis=1)
    rep = lambda t: jnp.concatenate([t, t], axis=1)
    return rep(c), rep(s_up), rep(s_dn)


def _rotary(x, c, s_up, s_dn):
    w = x.shape[1]
    half = ROT_DIM // 2
    reps = w // c.shape[1]
    t = lambda a: jnp.concatenate([a] * reps, axis=1)
    return (x * t(c) + pltpu.roll(x, half, axis=1) * t(s_up)
            + pltpu.roll(x, w - half, axis=1) * t(s_dn))


def _expand_kv_weights(w_qkv, b_qkv):
    nq = B_Q_HEADS * HEAD_DIM
    nk = B_KV_HEADS * HEAD_DIM

    def rep(a):
        lead = a.shape[:-1]
        a = a.reshape(lead + (B_KV_HEADS, 1, HEAD_DIM))
        a = jnp.broadcast_to(a, lead + (B_KV_HEADS, B_GROUP, HEAD_DIM))
        return a.reshape(lead + (nq,))

    w = jnp.concatenate([w_qkv[:, :nq], rep(w_qkv[:, nq:nq + nk]), rep(w_qkv[:, nq + nk:])], axis=1)
    bb = jnp.concatenate([b_qkv[:nq], rep(b_qkv[nq:nq + nk]), rep(b_qkv[nq + nk:])])
    return w, bb


def _attend(q, kwin, vwin, valid, sink_col):
    nq = q.shape[0]
    lane_head = lax.broadcasted_iota(jnp.int32, (1, GROUP_LANES), 1) // HEAD_DIM
    masks = [(lane_head == g).astype(_F32) for g in range(B_GROUP)]
    qs = jnp.concatenate([q * masks[g] for g in range(B_GROUP)], axis=0).astype(_BF16)
    s = lax.dot_general(qs, kwin, (((1,), (1,)), ((), ())), preferred_element_type=_F32)
    s = s * (HEAD_DIM ** -0.5)
    s = jnp.where(valid, s, -1e30)
    m = jnp.maximum(jnp.max(s, axis=-1, keepdims=True), sink_col)
    p = jnp.exp(s - m)
    p = p / (jnp.sum(p, axis=-1, keepdims=True) + jnp.exp(sink_col - m))
    o = _dot(p.astype(_BF16), vwin)
    out = o[0:nq] * masks[0]
    for g in range(1, B_GROUP):
        out = out + o[g * nq:(g + 1) * nq] * masks[g]
    return out


def _sink_cols(sink_ref, hk, nq):
    return jnp.concatenate(
        [jnp.full((nq, 1), sink_ref[hk * B_GROUP + g], _F32) for g in range(B_GROUP)], axis=0)


def _swa_prompt_kernel(sink_ref, x_ref, w_ref, bq_ref, c_ref, su_ref, sd_ref, w_out_ref, g_ref, b_ref,
                       o_ref, kc_ref, vc_ref, k_ext, v_ext):
    t = pl.program_id(1)
    tm = x_ref.shape[0]
    d = D_MODEL
    halo = BAND_CHUNKS * CHUNK

    @pl.when(t == 0)
    def _():
        k_ext[0:halo, :] = jnp.zeros((halo, d), _BF16)
        v_ext[0:halo, :] = jnp.zeros((halo, d), _BF16)

    x = x_ref[...]
    qkv = _dot(x.astype(_BF16), w_ref[...]) + bq_ref[...]
    qk = _rotary(qkv[:, :2 * d], c_ref[...], su_ref[...], sd_ref[...])
    q = qk[:, :d]
    k = qk[:, d:]
    v = qkv[:, 2 * d:]
    k_ext[halo:halo + tm, :] = k.astype(_BF16)
    v_ext[halo:halo + tm, :] = v.astype(_BF16)
    kc_ref[...] = k[tm - WINDOW:, :]
    vc_ref[...] = v[tm - WINDOW:, :]

    nkeys = (BAND_CHUNKS + 1) * CHUNK
    key_chunk = lax.broadcasted_iota(jnp.int32, (1, nkeys), 1) // CHUNK
    rows = []
    for n in range(tm // CHUNK):
        valid = (t * (tm // CHUNK) + n - BAND_CHUNKS + key_chunk) >= 0
        r0 = n * CHUNK
        cols = []
        for hk in range(B_KV_HEADS):
            l0 = hk * GROUP_LANES
            cols.append(_attend(q[r0:r0 + CHUNK, l0:l0 + GROUP_LANES],
                                k_ext[r0:r0 + nkeys, l0:l0 + GROUP_LANES],
                                v_ext[r0:r0 + nkeys, l0:l0 + GROUP_LANES],
                                valid, _sink_cols(sink_ref, hk, CHUNK)))
        rows.append(jnp.concatenate(cols, axis=1))
    att = jnp.concatenate(rows, axis=0)
    y = _dot(att.astype(_BF16), w_out_ref[...])
    o_ref[...] = _layer_norm(ALPHA * x + y, g_ref[...], b_ref[...])
    k_ext[0:halo, :] = k_ext[tm:tm + halo, :]
    v_ext[0:halo, :] = v_ext[tm:tm + halo, :]


def _unexpand(a):
    lead = a.shape[:-1]
    return a.reshape(lead + (B_KV_HEADS, B_GROUP, HEAD_DIM))[..., 0, :]


def _swa_prompt(x3, w_qkv, b_qkv, sink, w_out, g, b, *, tm):
    bsz, t, d = x3.shape
    w_exp, b_exp = _expand_kv_weights(w_qkv, b_qkv)
    c, su, sd = _rope_tables(jnp.arange(t))
    tab = pl.BlockSpec((tm, 2 * HEAD_DIM), lambda bi, ti, *_: (ti, 0))
    xspec = pl.BlockSpec((None, tm, d), lambda bi, ti, *_: (bi, ti, 0))
    cspec = pl.BlockSpec((None, WINDOW, d), lambda bi, ti, *_: (bi, 0, 0))
    halo = BAND_CHUNKS * CHUNK
    y, kc, vc = pl.pallas_call(
        _swa_prompt_kernel,
        grid_spec=pltpu.PrefetchScalarGridSpec(
            num_scalar_prefetch=1,
            grid=(bsz, t // tm),
            in_specs=[xspec, _full((d, 3 * d)), _full((1, 3 * d)), tab, tab, tab, _full((d, d)),
                      _full((1, d)), _full((1, d))],
            out_specs=[xspec, cspec, cspec],
            scratch_shapes=[pltpu.VMEM((tm + halo, d), _BF16), pltpu.VMEM((tm + halo, d), _BF16)],
        ),
        out_shape=[jax.ShapeDtypeStruct((bsz, t, d), _F32),
                   jax.ShapeDtypeStruct((bsz, WINDOW, d), _F32),
                   jax.ShapeDtypeStruct((bsz, WINDOW, d), _F32)],
        compiler_params=_cparams("arbitrary", "arbitrary"),
        name="swa_prompt_mixer",
    )(sink, x3, w_exp.astype(_BF16), b_exp[None], c, su, sd, w_out.astype(_BF16), g[None], b[None])
    return y, _unexpand(kc), _unexpand(vc)


def _swa_sample_kernel(sink_ref, x_ref, ck_ref, cv_ref, valid_ref, w_ref, bq_ref, c_ref, su_ref, sd_ref,
                       w_out_ref, g_ref, b_ref, o_ref, kn_ref, vn_ref, *, bsz, s):
    d = D_MODEL
    x = x_ref[...]
    qkv = _dot(x.astype(_BF16), w_ref[...]) + bq_ref[...]
    qk = _rotary(qkv[:, :2 * d], c_ref[...], su_ref[...], sd_ref[...])
    q = qk[:, :d]
    k = qk[:, d:]
    v = qkv[:, 2 * d:]
    kn_ref[...] = k
    vn_ref[...] = v
    kb = k.astype(_BF16)
    vb = v.astype(_BF16)
    valid = jnp.concatenate([valid_ref[...] > 0] * B_GROUP, axis=0)
    rows = []
    for bi in range(bsz):
        r0 = bi * s
        kk = jnp.concatenate([ck_ref[bi], kb[r0:r0 + s]], axis=0)
        vv = jnp.concatenate([cv_ref[bi], vb[r0:r0 + s]], axis=0)
        cols = []
        for hk in range(B_KV_HEADS):
            l0 = hk * GROUP_LANES
            cols.append(_attend(q[r0:r0 + s, l0:l0 + GROUP_LANES], kk[:, l0:l0 + GROUP_LANES],
                                vv[:, l0:l0 + GROUP_LANES], valid, _sink_cols(sink_ref, hk, s)))
        rows.append(jnp.concatenate(cols, axis=1))
    att = jnp.concatenate(rows, axis=0)
    y = _dot(att.astype(_BF16), w_out_ref[...])
    o_ref[...] = _layer_norm(ALPHA * x + y, g_ref[...], b_ref[...])


def _swa_sample(x3, ck, cv, w_qkv, b_qkv, sink, w_out, g, b):
    bsz, s, d = x3.shape
    nc = ck.shape[1]
    w_exp, b_exp = _expand_kv_weights(w_qkv, b_qkv)
    qpos = PAST_LEN + np.arange(s)
    kpos = np.concatenate([PAST_LEN - nc + np.arange(nc), qpos])
    qc, kc = qpos // CHUNK, kpos // CHUNK
    valid = ((kc[None, :] <= qc[:, None]) & (kc[None, :] >= qc[:, None] - BAND_CHUNKS)).astype(np.int32)
    c, su, sd = _rope_tables(jnp.tile(jnp.asarray(qpos), bsz))

    def expand_cache(a):
        a = jnp.broadcast_to(a[:, :, :, None, :], (bsz, nc, B_KV_HEADS, B_GROUP, HEAD_DIM))
        return a.reshape(bsz, nc, d).astype(_BF16)

    n = bsz * s
    vm = pl.BlockSpec(memory_space=pltpu.VMEM)
    y, kn, vn = pl.pallas_call(
        functools.partial(_swa_sample_kernel, bsz=bsz, s=s),
        in_specs=[pl.BlockSpec(memory_space=pltpu.SMEM)] + [vm] * 12,
        out_specs=[vm, vm, vm],
        out_shape=[jax.ShapeDtypeStruct((n, d), _F32)] * 3,
        compiler_params=pltpu.CompilerParams(vmem_limit_bytes=VMEM_LIMIT_BYTES),
        name="swa_sample_mixer",
    )(sink, x3.reshape(n, d), expand_cache(ck), expand_cache(cv), jnp.asarray(valid),
      w_exp.astype(_BF16), b_exp[None], c, su, sd, w_out.astype(_BF16), g[None], b[None])
    return (y.reshape(bsz, s, d), _unexpand(kn.reshape(bsz, s, d)), _unexpand(vn.reshape(bsz, s, d)))


def _conv_kernel(x_ref, st_ref, w_in_ref, b_in_ref, w_dw_ref, b_dw_ref, g_c_ref, b_c_ref, w_out_ref,
                 g_ref, b_ref, o_ref, st_out_ref, h_ext):
    t = pl.program_id(1)
    tm = x_ref.shape[0]
    d = D_MODEL

    @pl.when(t == 0)
    def _():
        h_ext[0:CONV_HALO, :] = st_ref[...]

    x = x_ref[...]
    ag = _dot(x.astype(_BF16), w_in_ref[...]) + b_in_ref[...]
    h = ag[:, :d] * _sigmoid(ag[:, d:])
    h_ext[CONV_HALO:CONV_HALO + tm, :] = h
    base = CONV_HALO - (CONV_WIDTH - 1)
    acc = jnp.zeros((tm, d), _F32)
    for sub in range(8):
        offs = [o for o in range(base, base + CONV_WIDTH) if o % 8 == sub]
        if not offs:
            continue
        span = max(offs) - sub + tm
        sh = h_ext[sub:sub + span, :]
        for o in offs:
            a0 = o - sub
            acc = acc + sh[a0:a0 + tm, :] * w_dw_ref[o - base:o - base + 1, :]
    y = acc + b_dw_ref[...]
    y = _layer_norm(y, g_c_ref[...], b_c_ref[...])
    y = y * _sigmoid(y)
    y = _dot(y.astype(_BF16), w_out_ref[...])
    o_ref[...] = _layer_norm(ALPHA * x + y, g_ref[...], b_ref[...])
    new_hist = h_ext[tm:tm + CONV_HALO, :]
    st_out_ref[...] = new_hist
    h_ext[0:CONV_HALO, :] = new_hist


def _conv(x3, state, w_in, b_in, w_dw, b_dw, g_c, b_c, w_out, g, b, *, tm):
    bsz, t, d = x3.shape
    st = jnp.pad(state, ((0, 0), (CONV_HALO - (CONV_WIDTH - 1), 0), (0, 0)))
    xspec = pl.BlockSpec((None, tm, d), lambda bi, ti: (bi, ti, 0))
    sspec = pl.BlockSpec((None, CONV_HALO, d), lambda bi, ti: (bi, 0, 0))
    y, st_new = pl.pallas_call(
        _conv_kernel,
        grid=(bsz, t // tm),
        in_specs=[xspec, sspec, _full((d, 2 * d)), _full((1, 2 * d)), _full((CONV_WIDTH, d)), _full((1, d)),
                  _full((1, d)), _full((1, d)), _full((d, d)), _full((1, d)), _full((1, d))],
        out_specs=[xspec, sspec],
        out_shape=[jax.ShapeDtypeStruct((bsz, t, d), _F32),
                   jax.ShapeDtypeStruct((bsz, CONV_HALO, d), _F32)],
        scratch_shapes=[pltpu.VMEM((tm + CONV_HALO, d), _F32)],
        compiler_params=_cparams("arbitrary", "arbitrary"),
        name="conv_mixer",
    )(x3, st, w_in.astype(_BF16), b_in[None], w_dw, b_dw[None], g_c[None], b_c[None],
      w_out.astype(_BF16), g[None], b[None])
    return y, st_new[:, CONV_HALO - (CONV_WIDTH - 1):]


def _swiglu_partial(xb, wg_ref, wu_ref, wo_ref, width):
    acc = None
    for c in range(width // FF_SUB):
        sl = slice(c * FF_SUB, (c + 1) * FF_SUB)
        gt = _dot(xb, wg_ref[:, sl])
        up = _dot(xb, wu_ref[:, sl])
        hh = (gt * _sigmoid(gt) * up).astype(_BF16)
        part = _dot(hh, wo_ref[sl, :])
        acc = part if acc is None else acc + part
    return acc


def _ffn_kernel(x_ref, wg_ref, wu_ref, wo_ref, g_ref, b_ref, o_ref, *, d_ff):
    x = x_ref[...]
    f = _swiglu_partial(x.astype(_BF16), wg_ref, wu_ref, wo_ref, d_ff)
    o_ref[...] = _layer_norm(ALPHA * x + f, g_ref[...], b_ref[...])


def _ffn(x2, w_in, w_out, g, b, *, tm):
    n, d = x2.shape
    d_ff = w_out.shape[0]
    w_in_b = w_in.astype(_BF16)
    one = pl.Buffered(1)
    return pl.pallas_call(
        functools.partial(_ffn_kernel, d_ff=d_ff),
        grid=(n // tm,),
        in_specs=[
            pl.BlockSpec((tm, d), lambda i: (i, 0)),
            pl.BlockSpec((d, d_ff), lambda i: (0, 0), pipeline_mode=one),
            pl.BlockSpec((d, d_ff), lambda i: (0, 1), pipeline_mode=one),
            pl.BlockSpec((d_ff, d), lambda i: (0, 0), pipeline_mode=one),
            _full((1, d)), _full((1, d)),
        ],
        out_specs=pl.BlockSpec((tm, d), lambda i: (i, 0)),
        out_shape=jax.ShapeDtypeStruct((n, d), _F32),
        compiler_params=_cparams("arbitrary"),
        name="dense_swiglu",
    )(x2, w_in_b, w_in_b, w_out.astype(_BF16), g[None], b[None])


def _moe_kernel(blk_e_ref, nblk_ref, x_ref, wg_ref, wu_ref, wo_ref, o_ref, acc_ref, *, width):
    bi = pl.program_id(0)
    j = pl.program_id(1)
    last = pl.num_programs(1) - 1
    live = bi < nblk_ref[0]

    @pl.when(live)
    def _():
        part = _swiglu_partial(x_ref[...].astype(_BF16), wg_ref, wu_ref, wo_ref, width)

        @pl.when(j == 0)
        def _():
            acc_ref[...] = part

        @pl.when((j > 0) & (j < last))
        def _():
            acc_ref[...] += part

        @pl.when(j == last)
        def _():
            o_ref[...] = acc_ref[...] + part

    @pl.when(jnp.logical_not(live) & (j == last))
    def _():
        o_ref[...] = jnp.zeros(o_ref.shape, o_ref.dtype)


def _moe_experts(xbuf, blk_e, nblk, w_e_in, w_e_out, *, blk):
    rows, d = xbuf.shape
    n_blocks = rows // blk
    d_ff = w_e_out.shape[1]
    splits = 2
    width = d_ff // splits

    def wj(i, j, nb):
        return jnp.where(i < nb[0], j, splits - 1)

    return pl.pallas_call(
        functools.partial(_moe_kernel, width=width),
        grid_spec=pltpu.PrefetchScalarGridSpec(
            num_scalar_prefetch=2,
            grid=(n_blocks, splits),
            in_specs=[
                pl.BlockSpec((blk, d), lambda i, j, e, nb: (i, 0)),
                pl.BlockSpec((None, d, width), lambda i, j, e, nb: (e[i], 0, wj(i, j, nb))),
                pl.BlockSpec((None, d, width), lambda i, j, e, nb: (e[i], 0, splits + wj(i, j, nb))),
                pl.BlockSpec((None, width, d), lambda i, j, e, nb: (e[i], wj(i, j, nb), 0)),
            ],
            out_specs=pl.BlockSpec((blk, d), lambda i, j, e, nb: (i, 0)),
            scratch_shapes=[pltpu.VMEM((blk, d), _F32)],
        ),
        out_shape=jax.ShapeDtypeStruct((rows, d), _F32),
        compiler_params=_cparams("arbitrary", "arbitrary"),
        name="moe_experts",
    )(blk_e, nblk, xbuf, w_e_in, w_e_in, w_e_out)


ROUTE_LANES = 128


def _router_kernel(x_ref, wh_ref, wl_ref, br_ref, info_ref, cnt_ref, base_ref):
    @pl.when(pl.program_id(0) == 0)
    def _():
        base_ref[...] = jnp.zeros(base_ref.shape, _F32)

    x = x_ref[...]
    tm = x.shape[0]
    xh = x.astype(_BF16)
    xl = (x - xh.astype(_F32)).astype(_BF16)
    logits = _dot(xh, wh_ref[...]) + (_dot(xh, wl_ref[...]) + _dot(xl, wh_ref[...])) + br_ref[...]
    lane = lax.broadcasted_iota(jnp.int32, logits.shape, 1)
    neg = -jnp.inf
    logits = jnp.where(lane < N_EXPERTS, logits, neg)
    m1 = jnp.max(logits, axis=-1, keepdims=True)
    i1 = jnp.min(jnp.where(logits == m1, lane, ROUTE_LANES), axis=-1, keepdims=True)
    rest = jnp.where(lane == i1, neg, logits)
    m2 = jnp.max(rest, axis=-1, keepdims=True)
    i2 = jnp.min(jnp.where(rest == m2, lane, ROUTE_LANES), axis=-1, keepdims=True)
    ex = jnp.exp(m2 - m1)
    g1 = 1.0 / (1.0 + ex)
    g2 = ex / (1.0 + ex)
    oh1 = (lane == i1).astype(_F32)
    oh2 = (lane == i2).astype(_F32)
    oh = oh1 + oh2
    tri = (lax.broadcasted_iota(jnp.int32, (tm, tm), 0) > lax.broadcasted_iota(jnp.int32, (tm, tm), 1))
    before = _dot(tri.astype(_BF16), oh.astype(_BF16)) + base_ref[...]
    r1 = jnp.sum(before * oh1, axis=-1, keepdims=True)
    r2 = jnp.sum(before * oh2, axis=-1, keepdims=True)
    base_ref[...] += jnp.sum(oh, axis=0, keepdims=True)
    cnt_ref[...] = base_ref[...]
    cols = (i1.astype(_F32), i2.astype(_F32), r1, r2, g1, g2)
    info = jnp.zeros(logits.shape, _F32)
    for c, val in enumerate(cols):
        info = jnp.where(lane == c, val, info)
    info_ref[...] = info


def _router(x2, w_r, b_r, *, tm):
    n, d = x2.shape
    wp = jnp.pad(w_r, ((0, 0), (0, ROUTE_LANES - N_EXPERTS)))
    wh = wp.astype(_BF16)
    wl = (wp - wh.astype(_F32)).astype(_BF16)
    bp = jnp.pad(b_r, (0, ROUTE_LANES - N_EXPERTS))[None]
    return pl.pallas_call(
        _router_kernel,
        grid=(n // tm,),
        in_specs=[pl.BlockSpec((tm, d), lambda i: (i, 0)), _full((d, ROUTE_LANES)), _full((d, ROUTE_LANES)),
                  _full((1, ROUTE_LANES))],
        out_specs=[pl.BlockSpec((tm, ROUTE_LANES), lambda i: (i, 0)), _full((1, ROUTE_LANES))],
        out_shape=[jax.ShapeDtypeStruct((n, ROUTE_LANES), _F32), jax.ShapeDtypeStruct((1, ROUTE_LANES), _F32)],
        scratch_shapes=[pltpu.VMEM((1, ROUTE_LANES), _F32)],
        compiler_params=_cparams("arbitrary"),
        name="moe_router",
    )(x2, wh, wl, bp)


def _route_plan(info, cnt, *, blk, n_blocks):
    e = info[:, 0:TOP_K].astype(jnp.int32)
    rank = info[:, TOP_K:2 * TOP_K].astype(jnp.int32)
    counts = cnt[0, :N_EXPERTS].astype(jnp.int32)
    padded = (counts + blk - 1) // blk * blk
    pad_end = jnp.cumsum(padded)
    pad_start = pad_end - padded
    dest = (pad_start[e] + rank).reshape(-1)
    nblk = pad_end[-1] // blk
    blk_e = jnp.minimum(jnp.searchsorted(pad_end, jnp.arange(n_blocks) * blk, side='right'), N_EXPERTS - 1)
    blk_e = blk_e[jnp.minimum(jnp.arange(n_blocks), nblk - 1)].astype(jnp.int32)
    n_fill = n_blocks * blk - dest.shape[0]
    tail = padded - counts
    gaps = jnp.cumsum(tail)
    slot = jnp.arange(n_fill)
    ge = jnp.minimum(jnp.searchsorted(gaps, slot, side='right'), N_EXPERTS - 1)
    in_gap = pad_start[ge] + counts[ge] + slot - (gaps[ge] - tail[ge])
    fill = jnp.where(slot < gaps[-1], in_gap, pad_end[-1] + slot - gaps[-1]).astype(jnp.int32)
    return dest.astype(jnp.int32), blk_e, nblk.astype(jnp.int32)[None], fill


def _row_copy(src_ref, src_row, dst_ref, dst_row, sem):
    return pltpu.make_async_copy(src_ref.at[pl.ds(src_row, 1)], dst_ref.at[pl.ds(dst_row, 1)], sem)


def _dispatch_kernel(fill_ref, dest_ref, x_ref, xbuf_ref, zero_ref, sem, zsem, *, n_fill):
    tm = x_ref.shape[0]
    zrows = zero_ref.shape[0]

    def issue(r, c):
        for k in range(TOP_K):
            _row_copy(x_ref, r, xbuf_ref, dest_ref[TOP_K * r + k], sem).start()
        return c

    lax.fori_loop(0, tm, issue, 0, unroll=8)

    @pl.when(pl.program_id(0) == pl.num_programs(0) - 1)
    def _():
        zero_ref[...] = jnp.zeros(zero_ref.shape, zero_ref.dtype)

        def zissue(r, c):
            _row_copy(zero_ref, 0, xbuf_ref, fill_ref[r], zsem).start()
            return c

        lax.fori_loop(0, n_fill, zissue, 0, unroll=8)

        def zwait(r, c):
            pltpu.make_async_copy(zero_ref, xbuf_ref.at[pl.ds(0, zrows)], zsem).wait()
            return c

        lax.fori_loop(0, n_fill // zrows, zwait, 0)

    for k in range(TOP_K):
        pltpu.make_async_copy(x_ref, xbuf_ref.at[pl.ds(0, tm)], sem).wait()


def _dispatch(x2, dest, fill, *, rows, tm):
    n, d = x2.shape
    n_fill = fill.shape[0]
    zrows = 8
    assert n_fill % zrows == 0
    return pl.pallas_call(
        functools.partial(_dispatch_kernel, n_fill=n_fill),
        grid_spec=pltpu.PrefetchScalarGridSpec(
            num_scalar_prefetch=1,
            grid=(n // tm,),
            in_specs=[pl.BlockSpec((TOP_K * tm,), lambda i, f: (i,), memory_space=pltpu.SMEM),
                      pl.BlockSpec((tm, d), lambda i, f: (i, 0))],
            out_specs=pl.BlockSpec(memory_space=pl.ANY),
            scratch_shapes=[pltpu.VMEM((zrows, d), _F32), pltpu.SemaphoreType.DMA, pltpu.SemaphoreType.DMA],
        ),
        out_shape=jax.ShapeDtypeStruct((rows, d), _F32),
        compiler_params=_cparams("arbitrary"),
        name="moe_dispatch",
    )(fill, dest, x2)


def _combine_kernel(dest_ref, x_ref, info_ref, ybuf_ref, g_ref, b_ref, o_ref, y_ref, sems):
    tm = x_ref.shape[0]

    def issue(r, c):
        for k in range(TOP_K):
            _row_copy(ybuf_ref, dest_ref[TOP_K * r + k], y_ref.at[k], r, sems.at[k]).start()
        return c

    lax.fori_loop(0, tm, issue, 0, unroll=8)
    for k in range(TOP_K):
        pltpu.make_async_copy(ybuf_ref.at[pl.ds(0, tm)], y_ref.at[k], sems.at[k]).wait()
    info = info_ref[...]
    f = y_ref[0] * info[:, 2 * TOP_K:2 * TOP_K + 1]
    for k in range(1, TOP_K):
        f = f + y_ref[k] * info[:, 2 * TOP_K + k:2 * TOP_K + k + 1]
    o_ref[...] = _layer_norm(ALPHA * x_ref[...] + f, g_ref[...], b_ref[...])


def _combine(x2, info, dest, ybuf, g, b, *, tm):
    n, d = x2.shape
    row = pl.BlockSpec((tm, d), lambda i: (i, 0))
    return pl.pallas_call(
        _combine_kernel,
        grid=(n // tm,),
        in_specs=[pl.BlockSpec((TOP_K * tm,), lambda i: (i,), memory_space=pltpu.SMEM), row,
                  pl.BlockSpec((tm, ROUTE_LANES), lambda i: (i, 0)), pl.BlockSpec(memory_space=pl.ANY),
                  _full((1, d)), _full((1, d))],
        out_specs=row,
        out_shape=jax.ShapeDtypeStruct((n, d), _F32),
        scratch_shapes=[pltpu.VMEM((TOP_K, tm, d), _F32), pltpu.SemaphoreType.DMA((TOP_K,))],
        compiler_params=_cparams("arbitrary"),
        name="moe_combine",
    )(dest, x2, info, ybuf, g[None], b[None])


def _moe(x2, w_r, b_r, w_e_in, w_e_out, g, b, *, tm):
    n, d = x2.shape
    n_assign = n * TOP_K
    blk = min(MOE_BLOCK, max(8, -(-n_assign // (8 * N_EXPERTS)) * 8))
    n_blocks = -(-(n_assign + N_EXPERTS * (blk - 1)) // blk)
    info, cnt = _router(x2, w_r, b_r, tm=tm)
    dest, blk_e, nblk, fill = _route_plan(info, cnt, blk=blk, n_blocks=n_blocks)
    xbuf = _dispatch(x2, dest, fill, rows=n_blocks * blk, tm=tm)
    ybuf = _moe_experts(xbuf, blk_e, nblk, w_e_in.astype(_BF16), w_e_out.astype(_BF16), blk=blk)
    return _combine(x2, info, dest, ybuf, g, b, tm=tm)


def kernel(x_prompt, x_sample, cache_b_k, cache_b_v, state_c_conv, ln_g, ln_b, w_a_in, b_a_in, ln_a_g, ln_a_b,
           w_a_s, b_a_s, w_a_out, w_b_qkv, b_b_qkv, b_sink, w_b_out, w_c_in, b_c_in, w_c_dw, b_c_dw, ln_c_g,
           ln_c_b, w_c_out, w_f_in, w_f_out, w_r, b_r, w_e_in, w_e_out):
    bsz, t, d = x_prompt.shape
    dbs, s, _ = x_sample.shape
    n_p, n_s = bsz * t, dbs * s
    xp = x_prompt.reshape(n_p, d)
    xs = x_sample.reshape(n_s, d)
    a_v_s, b_k_p, b_v_p, b_k_s, b_v_s, c_p, c_s = [], [], [], [], [], [], []
    for i in range(DEPTH):
        kind, j = i % N_MIXERS, i // N_MIXERS
        g0, b0 = ln_g[i, 0], ln_b[i, 0]
        if kind == 0:
            args = (w_a_in[j], b_a_in[j], ln_a_g[j], ln_a_b[j], w_a_s[j], b_a_s[j], w_a_out[j], g0, b0)
            (xp,) = _gmlp(xp, *args, chunk=A_CHUNK, tm=ROW_TILE, emit_v=False)
            xs, v_rows = _gmlp(xs, *args, chunk=s, tm=n_s, emit_v=True)
            a_v_s.append(v_rows.reshape(dbs, s, d))
        elif kind == 1:
            args = (w_b_qkv[j], b_b_qkv[j], b_sink[j], w_b_out[j], g0, b0)
            xp3, kp, vp = _swa_prompt(xp.reshape(bsz, t, d), *args, tm=ROW_TILE)
            xs3, kn, vn = _swa_sample(xs.reshape(dbs, s, d), cache_b_k[j], cache_b_v[j], *args)
            xp, xs = xp3.reshape(n_p, d), xs3.reshape(n_s, d)
            b_k_p.append(kp)
            b_v_p.append(vp)
            b_k_s.append(kn)
            b_v_s.append(vn)
        else:
            args = (w_c_in[j], b_c_in[j], w_c_dw[j], b_c_dw[j], ln_c_g[j], ln_c_b[j], w_c_out[j], g0, b0)
            zero_state = jnp.zeros((bsz, CONV_WIDTH - 1, d), _F32)
            xp3, cp = _conv(xp.reshape(bsz, t, d), zero_state, *args, tm=ROW_TILE)
            xs3, cs = _conv(xs.reshape(dbs, s, d), state_c_conv[j], *args, tm=s)
            xp, xs = xp3.reshape(n_p, d), xs3.reshape(n_s, d)
            c_p.append(cp)
            c_s.append(cs)
        f = i // 2
        g1, b1 = ln_g[i, 1], ln_b[i, 1]
        if i % 2 == 0:
            xp = _ffn(xp, w_f_in[f], w_f_out[f], g1, b1, tm=ROW_TILE)
            xs = _ffn(xs, w_f_in[f], w_f_out[f], g1, b1, tm=n_s)
        else:
            xp = _moe(xp, w_r[f], b_r[f], w_e_in[f], w_e_out[f], g1, b1, tm=ROW_TILE)
            xs = _moe(xs, w_r[f], b_r[f], w_e_in[f], w_e_out[f], g1, b1, tm=n_s)
    return (xp.reshape(bsz, t, d), xs.reshape(dbs, s, d), jnp.stack(a_v_s), jnp.stack(b_k_p),
            jnp.stack(b_v_p), jnp.stack(b_k_s), jnp.stack(b_v_s), jnp.stack(c_p), jnp.stack(c_s))
```

```python
import functools

import jax
import jax.numpy as jnp
import numpy as np
from jax import lax
from jax.experimental import pallas as pl
from jax.experimental.pallas import tpu as pltpu

D_MODEL = 1024
DEPTH = 4
PAST_LEN = 1024
CHUNK = 64
N_MIXERS = 3
ALPHA = (2 * DEPTH) ** 0.25
LN_EPS = 1e-5
A_CHUNK = 128
A_GROUPS = 4
HEAD_DIM = 64
B_Q_HEADS = D_MODEL // HEAD_DIM
B_KV_HEADS = 4
B_GROUP = B_Q_HEADS // B_KV_HEADS
WINDOW = 128
BAND_CHUNKS = WINDOW // CHUNK
ROT_DIM = HEAD_DIM // 4
ROPE_THETA = 500000.0
CONV_WIDTH = 31
N_EXPERTS = 8
TOP_K = 2
MOE_BLOCK = 512

VMEM_LIMIT_BYTES = 56 * 1024 * 1024
ROW_TILE = 512
FF_SUB = 256
CONV_HALO = 32
GROUP_LANES = B_GROUP * HEAD_DIM
ROPE_LANES = 128

_BF16 = jnp.bfloat16
_F32 = jnp.float32


def _cparams(*sem):
    return pltpu.CompilerParams(dimension_semantics=sem, vmem_limit_bytes=VMEM_LIMIT_BYTES)


def _dot(a, b):
    return jnp.dot(a, b, preferred_element_type=_F32)


def _layer_norm(x, g, b):
    mu = jnp.mean(x, axis=-1, keepdims=True)
    xc = x - mu
    var = jnp.mean(xc * xc, axis=-1, keepdims=True)
    return xc * lax.rsqrt(var + LN_EPS) * g + b


def _sigmoid(x):
    return 1.0 / (1.0 + jnp.exp(-x))


def _full(shape):
    nd = len(shape)
    return pl.BlockSpec(shape, lambda *_: (0,) * nd)


def _gmlp_kernel(x_ref, w_in_ref, b_in_ref, g_a_ref, b_a_ref, w_s_ref, bz_ref, w_out_ref,
                 g_ref, b_ref, *out_refs, chunk, emit_v):
    o_ref = out_refs[0]
    x = x_ref[...]
    tm = x.shape[0]
    h = _dot(x.astype(_BF16), w_in_ref[...]) + b_in_ref[...]
    h = jax.nn.gelu(h)
    u = h[:, :D_MODEL]
    v = _layer_norm(h[:, D_MODEL:], g_a_ref[...], b_a_ref[...])
    if emit_v:
        out_refs[1][...] = v
    vb = v.astype(_BF16)
    ri = lax.broadcasted_iota(jnp.int32, (chunk, chunk), 0) // CHUNK
    ci = lax.broadcasted_iota(jnp.int32, (chunk, chunk), 1) // CHUNK
    gw = D_MODEL // A_GROUPS
    ws = [jnp.where(ci <= ri, w_s_ref[g], 0.0).astype(_BF16) for g in range(A_GROUPS)]
    rows = []
    for n in range(tm // chunk):
        r0 = n * chunk
        cols = [_dot(ws[g], vb[r0:r0 + chunk, g * gw:(g + 1) * gw]) for g in range(A_GROUPS)]
        rows.append(jnp.concatenate(cols, axis=1) + bz_ref[...])
    z = jnp.concatenate(rows, axis=0) if len(rows) > 1 else rows[0]
    y = _dot((u * z).astype(_BF16), w_out_ref[...])
    o_ref[...] = _layer_norm(ALPHA * x + y, g_ref[...], b_ref[...])


def _gmlp(x2, w_in, b_in, g_a, b_a, w_s, b_s, w_out, g, b, *, chunk, tm, emit_v):
    n = x2.shape[0]
    d = D_MODEL
    gw = d // A_GROUPS
    bz = jnp.repeat(b_s[:, :chunk].T, gw, axis=1)
    out_shape = [jax.ShapeDtypeStruct((n, d), _F32)]
    out_specs = [pl.BlockSpec((tm, d), lambda i: (i, 0))]
    if emit_v:
        out_shape.append(jax.ShapeDtypeStruct((n, d), _F32))
        out_specs.append(pl.BlockSpec((tm, d), lambda i: (i, 0)))
    res = pl.pallas_call(
        functools.partial(_gmlp_kernel, chunk=chunk, emit_v=emit_v),
        grid=(n // tm,),
        in_specs=[
            pl.BlockSpec((tm, d), lambda i: (i, 0)),
            _full((d, 2 * d)), _full((1, 2 * d)), _full((1, d)), _full((1, d)),
            _full((A_GROUPS, chunk, chunk)), _full((chunk, d)), _full((d, d)),
            _full((1, d)), _full((1, d)),
        ],
        out_specs=out_specs,
        out_shape=out_shape,
        compiler_params=_cparams("arbitrary"),
        name="gmlp_mixer",
    )(x2, w_in.astype(_BF16), b_in[None], g_a[None], b_a[None], w_s[:, :chunk, :chunk], bz,
      w_out.astype(_BF16), g[None], b[None])
    return res


def _rope_tables(pos):
    half = ROT_DIM // 2
    inv = ROPE_THETA ** (-np.arange(0, ROT_DIM, 2, dtype=np.float32) / ROT_DIM)
    lane = np.arange(ROPE_LANES) % HEAD_DIM
    inv_lane = jnp.asarray(inv[lane % half], _F32)[None, :]
    lo = jnp.asarray(lane < half)[None, :]
    hi = jnp.asarray((lane >= half) & (lane < ROT_DIM))[None, :]
    ang = pos.astype(_F32)[:, None] * inv_lane
    cos, sin = jnp.cos(ang), jnp.sin(ang)
    c = jnp.where(lo | hi, cos, 1.0)
    s_up = jnp.where(hi, sin, 0.0)
    s_dn = jnp.where(lo, -sin, 0.0)
    return c, s_up, s_dn


def _rotary(x, c, s_up, s_dn):
    half = ROT_DIM // 2
    slabs = []
    for i in range(x.shape[1] // ROPE_LANES):
        xs = x[:, i * ROPE_LANES:(i + 1) * ROPE_LANES]
        slabs.append(xs * c + pltpu.roll(xs, half, axis=1) * s_up
                     + pltpu.roll(xs, ROPE_LANES - half, axis=1) * s_dn)
    return jnp.concatenate(slabs, axis=1)


def _expand_kv_weights(w_qkv, b_qkv):
    nq = B_Q_HEADS * HEAD_DIM
    nk = B_KV_HEADS * HEAD_DIM
    scale = HEAD_DIM ** -0.5
    assert float(np.log2(scale)).is_integer()

    def rep(a):
        lead = a.shape[:-1]
        a = a.reshape(lead + (B_KV_HEADS, 1, HEAD_DIM))
        a = jnp.broadcast_to(a, lead + (B_KV_HEADS, B_GROUP, HEAD_DIM))
        return a.reshape(lead + (nq,))

    w = jnp.concatenate([w_qkv[:, :nq] * scale, rep(w_qkv[:, nq:nq + nk]), rep(w_qkv[:, nq + nk:])], axis=1)
    bb = jnp.concatenate([b_qkv[:nq] * scale, rep(b_qkv[nq:nq + nk]), rep(b_qkv[nq + nk:])])
    return w, bb


def _attend(q, kwin, vwin, valid, sink_col):
    nq = q.shape[0]
    lane_head = lax.broadcasted_iota(jnp.int32, (1, GROUP_LANES), 1) // HEAD_DIM
    masks = [(lane_head == g).astype(_F32) for g in range(B_GROUP)]
    qs = jnp.concatenate([q * masks[g] for g in range(B_GROUP)], axis=0).astype(_BF16)
    s = lax.dot_general(qs, kwin, (((1,), (1,)), ((), ())), preferred_element_type=_F32)
    if valid is not None:
        s = jnp.where(valid, s, -1e30)
    m = jnp.maximum(jnp.max(s, axis=-1, keepdims=True), sink_col)
    p = jnp.exp(s - m)
    p = p / (jnp.sum(p, axis=-1, keepdims=True) + jnp.exp(sink_col - m))
    o = _dot(p.astype(_BF16), vwin)
    out = o[0:nq] * masks[0]
    for g in range(1, B_GROUP):
        out = out + o[g * nq:(g + 1) * nq] * masks[g]
    return out


def _sink_cols(sink_ref, hk, nq):
    return jnp.concatenate(
        [jnp.full((nq, 1), sink_ref[hk * B_GROUP + g], _F32) for g in range(B_GROUP)], axis=0)


SCORE_LANES = 256
SOFTMAX_ROWS = 512


def _swa_prompt_kernel(sink_ref, x_ref, w_ref, bq_ref, c_ref, su_ref, sd_ref, w_out_ref, g_ref, b_ref,
                       o_ref, kc_ref, vc_ref, k_ext, v_ext, s_ref, p_ref):
    t = pl.program_id(1)
    tm = x_ref.shape[0]
    d = D_MODEL
    halo = BAND_CHUNKS * CHUNK
    nkeys = (BAND_CHUNKS + 1) * CHUNK
    n_chunks = tm // CHUNK
    grp = B_GROUP * CHUNK

    @pl.when(t == 0)
    def _():
        k_ext[0:halo, :] = jnp.zeros((halo, d), _BF16)
        v_ext[0:halo, :] = jnp.zeros((halo, d), _BF16)
        k_ext[halo + tm:, :] = jnp.zeros((k_ext.shape[0] - halo - tm, d), _BF16)

    x = x_ref[...]
    qkv = _dot(x.astype(_BF16), w_ref[...]) + bq_ref[...]
    qk = _rotary(qkv[:, :2 * d], c_ref[...], su_ref[...], sd_ref[...])
    q = qk[:, :d]
    k = qk[:, d:]
    v = qkv[:, 2 * d:]
    k_ext[halo:halo + tm, :] = k.astype(_BF16)
    v_ext[halo:halo + tm, :] = v.astype(_BF16)
    kc_ref[...] = k[tm - WINDOW:, :]
    vc_ref[...] = v[tm - WINDOW:, :]

    lane_head = lax.broadcasted_iota(jnp.int32, (1, GROUP_LANES), 1) // HEAD_DIM
    masks = [(lane_head == g).astype(_F32) for g in range(B_GROUP)]
    key_lane = lax.broadcasted_iota(jnp.int32, (grp, SCORE_LANES), 1)
    tails = [jnp.where(key_lane == nkeys, _sink_cols(sink_ref, hk, CHUNK), -1e30) for hk in range(B_KV_HEADS)]

    for n in range(n_chunks):
        r0 = n * CHUNK
        if n < BAND_CHUNKS:
            first = jnp.maximum((BAND_CHUNKS - n - t * n_chunks) * CHUNK, 0)
            keep = (lax.bitcast_convert_type(key_lane - first, jnp.uint32)
                    < lax.bitcast_convert_type(nkeys - first, jnp.uint32))
        else:
            keep = key_lane < nkeys
        for hk in range(B_KV_HEADS):
            l0 = hk * GROUP_LANES
            qn = q[r0:r0 + CHUNK, l0:l0 + GROUP_LANES]
            qs = jnp.concatenate([qn * masks[g] for g in range(B_GROUP)], axis=0).astype(_BF16)
            s = lax.dot_general(qs, k_ext[r0:r0 + SCORE_LANES, l0:l0 + GROUP_LANES],
                                (((1,), (1,)), ((), ())), preferred_element_type=_F32)
            i0 = (n * B_KV_HEADS + hk) * grp
            s_ref[i0:i0 + grp, :] = jnp.where(keep, s, tails[hk])

    def softmax_rows(i, c):
        rows = pl.ds(pl.multiple_of(i * SOFTMAX_ROWS, SOFTMAX_ROWS), SOFTMAX_ROWS)
        s = s_ref[rows, :]
        p = jnp.exp(s - jnp.max(s, axis=-1, keepdims=True))
        p_ref[rows, :] = (p / jnp.sum(p, axis=-1, keepdims=True)).astype(_BF16)
        return c

    lax.fori_loop(0, s_ref.shape[0] // SOFTMAX_ROWS, softmax_rows, 0)

    rows = []
    for n in range(n_chunks):
        r0 = n * CHUNK
        cols = []
        for hk in range(B_KV_HEADS):
            l0 = hk * GROUP_LANES
            i0 = (n * B_KV_HEADS + hk) * grp
            o = _dot(p_ref[i0:i0 + grp, 0:nkeys], v_ext[r0:r0 + nkeys, l0:l0 + GROUP_LANES])
            out = o[0:CHUNK] * masks[0]
            for g in range(1, B_GROUP):
                out = out + o[g * CHUNK:(g + 1) * CHUNK] * masks[g]
            cols.append(out)
        rows.append(jnp.concatenate(cols, axis=1))
    att = jnp.concatenate(rows, axis=0)
    y = _dot(att.astype(_BF16), w_out_ref[...])
    o_ref[...] = _layer_norm(ALPHA * x + y, g_ref[...], b_ref[...])
    k_ext[0:halo, :] = k_ext[tm:tm + halo, :]
    v_ext[0:halo, :] = v_ext[tm:tm + halo, :]


def _unexpand(a):
    lead = a.shape[:-1]
    return a.reshape(lead + (B_KV_HEADS, B_GROUP, HEAD_DIM))[..., 0, :]


def _swa_prompt(x3, w_qkv, b_qkv, sink, w_out, g, b, *, tm):
    bsz, t, d = x3.shape
    w_exp, b_exp = _expand_kv_weights(w_qkv, b_qkv)
    c, su, sd = _rope_tables(jnp.arange(t))
    tab = pl.BlockSpec((tm, 2 * HEAD_DIM), lambda bi, ti, *_: (ti, 0))
    xspec = pl.BlockSpec((None, tm, d), lambda bi, ti, *_: (bi, ti, 0))
    cspec = pl.BlockSpec((None, WINDOW, d), lambda bi, ti, *_: (bi, 0, 0))
    halo = BAND_CHUNKS * CHUNK
    score_rows = (tm // CHUNK) * B_KV_HEADS * B_GROUP * CHUNK
    assert score_rows % SOFTMAX_ROWS == 0
    one = pl.Buffered(1)
    y, kc, vc = pl.pallas_call(
        _swa_prompt_kernel,
        grid_spec=pltpu.PrefetchScalarGridSpec(
            num_scalar_prefetch=1,
            grid=(bsz, t // tm),
            in_specs=[xspec, pl.BlockSpec((d, 3 * d), lambda *_: (0, 0), pipeline_mode=one),
                      _full((1, 3 * d)), tab, tab, tab,
                      pl.BlockSpec((d, d), lambda *_: (0, 0), pipeline_mode=one),
                      _full((1, d)), _full((1, d))],
            out_specs=[xspec, cspec, cspec],
            scratch_shapes=[
                pltpu.VMEM((halo + tm + SCORE_LANES - (halo + CHUNK), d), _BF16),
                pltpu.VMEM((halo + tm, d), _BF16),
                pltpu.VMEM((score_rows, SCORE_LANES), _F32),
                pltpu.VMEM((score_rows, SCORE_LANES), _BF16)],
        ),
        out_shape=[jax.ShapeDtypeStruct((bsz, t, d), _F32),
                   jax.ShapeDtypeStruct((bsz, WINDOW, d), _F32),
                   jax.ShapeDtypeStruct((bsz, WINDOW, d), _F32)],
        compiler_params=_cparams("arbitrary", "arbitrary"),
        name="swa_prompt_mixer",
    )(sink, x3, w_exp.astype(_BF16), b_exp[None], c, su, sd, w_out.astype(_BF16), g[None], b[None])
    return y, _unexpand(kc), _unexpand(vc)


def _swa_sample_kernel(sink_ref, x_ref, ck_ref, cv_ref, valid_ref, w_ref, bq_ref, c_ref, su_ref, sd_ref,
                       w_out_ref, g_ref, b_ref, o_ref, kn_ref, vn_ref, *, bsz, s):
    d = D_MODEL
    x = x_ref[...]
    qkv = _dot(x.astype(_BF16), w_ref[...]) + bq_ref[...]
    qk = _rotary(qkv[:, :2 * d], c_ref[...], su_ref[...], sd_ref[...])
    q = qk[:, :d]
    k = qk[:, d:]
    v = qkv[:, 2 * d:]
    kn_ref[...] = k
    vn_ref[...] = v
    kb = k.astype(_BF16)
    vb = v.astype(_BF16)
    valid = jnp.concatenate([valid_ref[...] > 0] * B_GROUP, axis=0)
    rows = []
    for bi in range(bsz):
        r0 = bi * s
        kk = jnp.concatenate([ck_ref[bi], kb[r0:r0 + s]], axis=0)
        vv = jnp.concatenate([cv_ref[bi], vb[r0:r0 + s]], axis=0)
        cols = []
        for hk in range(B_KV_HEADS):
            l0 = hk * GROUP_LANES
            cols.append(_attend(q[r0:r0 + s, l0:l0 + GROUP_LANES], kk[:, l0:l0 + GROUP_LANES],
                                vv[:, l0:l0 + GROUP_LANES], valid, _sink_cols(sink_ref, hk, s)))
        rows.append(jnp.concatenate(cols, axis=1))
    att = jnp.concatenate(rows, axis=0)
    y = _dot(att.astype(_BF16), w_out_ref[...])
    o_ref[...] = _layer_norm(ALPHA * x + y, g_ref[...], b_ref[...])


def _swa_sample(x3, ck, cv, w_qkv, b_qkv, sink, w_out, g, b):
    bsz, s, d = x3.shape
    nc = ck.shape[1]
    w_exp, b_exp = _expand_kv_weights(w_qkv, b_qkv)
    qpos = PAST_LEN + np.arange(s)
    kpos = np.concatenate([PAST_LEN - nc + np.arange(nc), qpos])
    qc, kc = qpos // CHUNK, kpos // CHUNK
    valid = ((kc[None, :] <= qc[:, None]) & (kc[None, :] >= qc[:, None] - BAND_CHUNKS)).astype(np.int32)
    c, su, sd = _rope_tables(jnp.tile(jnp.asarray(qpos), bsz))

    def expand_cache(a):
        a = jnp.broadcast_to(a[:, :, :, None, :], (bsz, nc, B_KV_HEADS, B_GROUP, HEAD_DIM))
        return a.reshape(bsz, nc, d).astype(_BF16)

    n = bsz * s
    vm = pl.BlockSpec(memory_space=pltpu.VMEM)
    y, kn, vn = pl.pallas_call(
        functools.partial(_swa_sample_kernel, bsz=bsz, s=s),
        in_specs=[pl.BlockSpec(memory_space=pltpu.SMEM)] + [vm] * 12,
        out_specs=[vm, vm, vm],
        out_shape=[jax.ShapeDtypeStruct((n, d), _F32)] * 3,
        compiler_params=pltpu.CompilerParams(vmem_limit_bytes=VMEM_LIMIT_BYTES),
        name="swa_sample_mixer",
    )(sink, x3.reshape(n, d), expand_cache(ck), expand_cache(cv), jnp.asarray(valid),
      w_exp.astype(_BF16), b_exp[None], c, su, sd, w_out.astype(_BF16), g[None], b[None])
    return (y.reshape(bsz, s, d), _unexpand(kn.reshape(bsz, s, d)), _unexpand(vn.reshape(bsz, s, d)))


def _conv_kernel(x_ref, st_ref, w_in_ref, b_in_ref, w_dw_ref, b_dw_ref, g_c_ref, b_c_ref, w_out_ref,
                 g_ref, b_ref, o_ref, st_out_ref, h_ext):
    t = pl.program_id(1)
    tm = x_ref.shape[0]
    d = D_MODEL

    @pl.when(t == 0)
    def _():
        h_ext[0:CONV_HALO, :] = st_ref[...]

    x = x_ref[...]
    ag = _dot(x.astype(_BF16), w_in_ref[...]) + b_in_ref[...]
    h = ag[:, :d] * _sigmoid(ag[:, d:])
    h_ext[CONV_HALO:CONV_HALO + tm, :] = h
    base = CONV_HALO - (CONV_WIDTH - 1)
    acc = None
    for sub in range(8):
        offs = [o for o in range(base, base + CONV_WIDTH) if o % 8 == sub]
        if not offs:
            continue
        rows = tm + (8 if sub else 0)
        part = None
        for o in offs:
            a0 = o - sub
            term = h_ext[a0:a0 + rows, :] * w_dw_ref[o - base:o - base + 1, :]
            part = term if part is None else part + term
        part = part[sub:sub + tm, :]
        acc = part if acc is None else acc + part
    y = acc + b_dw_ref[...]
    y = _layer_norm(y, g_c_ref[...], b_c_ref[...])
    y = y * _sigmoid(y)
    y = _dot(y.astype(_BF16), w_out_ref[...])
    o_ref[...] = _layer_norm(ALPHA * x + y, g_ref[...], b_ref[...])
    new_hist = h_ext[tm:tm + CONV_HALO, :]
    st_out_ref[...] = new_hist
    h_ext[0:CONV_HALO, :] = new_hist


def _conv(x3, state, w_in, b_in, w_dw, b_dw, g_c, b_c, w_out, g, b, *, tm):
    bsz, t, d = x3.shape
    st = jnp.pad(state, ((0, 0), (CONV_HALO - (CONV_WIDTH - 1), 0), (0, 0)))
    xspec = pl.BlockSpec((None, tm, d), lambda bi, ti: (bi, ti, 0))
    sspec = pl.BlockSpec((None, CONV_HALO, d), lambda bi, ti: (bi, 0, 0))
    y, st_new = pl.pallas_call(
        _conv_kernel,
        grid=(bsz, t // tm),
        in_specs=[xspec, sspec, _full((d, 2 * d)), _full((1, 2 * d)), _full((CONV_WIDTH, d)), _full((1, d)),
                  _full((1, d)), _full((1, d)), _full((d, d)), _full((1, d)), _full((1, d))],
        out_specs=[xspec, sspec],
        out_shape=[jax.ShapeDtypeStruct((bsz, t, d), _F32),
                   jax.ShapeDtypeStruct((bsz, CONV_HALO, d), _F32)],
        scratch_shapes=[pltpu.VMEM((tm + CONV_HALO, d), _F32)],
        compiler_params=_cparams("arbitrary", "arbitrary"),
        name="conv_mixer",
    )(x3, st, w_in.astype(_BF16), b_in[None], w_dw, b_dw[None], g_c[None], b_c[None],
      w_out.astype(_BF16), g[None], b[None])
    return y, st_new[:, CONV_HALO - (CONV_WIDTH - 1):]


def _swiglu_partial(xb, wg_ref, wu_ref, wo_ref, width):
    acc = None
    for c in range(width // FF_SUB):
        sl = slice(c * FF_SUB, (c + 1) * FF_SUB)
        gt = _dot(xb, wg_ref[:, sl])
        up = _dot(xb, wu_ref[:, sl])
        hh = (gt * _sigmoid(gt) * up).astype(_BF16)
        part = _dot(hh, wo_ref[sl, :])
        acc = part if acc is None else acc + part
    return acc


def _ffn_kernel(x_ref, wg_ref, wu_ref, wo_ref, g_ref, b_ref, o_ref, *, d_ff):
    x = x_ref[...]
    f = _swiglu_partial(x.astype(_BF16), wg_ref, wu_ref, wo_ref, d_ff)
    o_ref[...] = _layer_norm(ALPHA * x + f, g_ref[...], b_ref[...])


def _ffn(x2, w_in, w_out, g, b, *, layer, tm):
    n, d = x2.shape
    d_ff = w_out.shape[1]
    one = pl.Buffered(1)
    return pl.pallas_call(
        functools.partial(_ffn_kernel, d_ff=d_ff),
        grid=(n // tm,),
        in_specs=[
            pl.BlockSpec((tm, d), lambda i: (i, 0)),
            pl.BlockSpec((None, d, d_ff), lambda i: (layer, 0, 0), pipeline_mode=one),
            pl.BlockSpec((None, d, d_ff), lambda i: (layer, 0, 1), pipeline_mode=one),
            pl.BlockSpec((None, d_ff, d), lambda i: (layer, 0, 0), pipeline_mode=one),
            _full((1, d)), _full((1, d)),
        ],
        out_specs=pl.BlockSpec((tm, d), lambda i: (i, 0)),
        out_shape=jax.ShapeDtypeStruct((n, d), _F32),
        compiler_params=_cparams("arbitrary"),
        name="dense_swiglu",
    )(x2, w_in, w_in, w_out, g[None], b[None])


def _moe_kernel(blk_e_ref, nblk_ref, x_ref, wg_ref, wu_ref, wo_ref, o_ref, acc_ref, *, width):
    bi = pl.program_id(0)
    j = pl.program_id(1)
    last = pl.num_programs(1) - 1
    live = bi < nblk_ref[0]

    @pl.when(live)
    def _():
        part = _swiglu_partial(x_ref[...].astype(_BF16), wg_ref, wu_ref, wo_ref, width)

        @pl.when(j == 0)
        def _():
            acc_ref[...] = part

        @pl.when((j > 0) & (j < last))
        def _():
            acc_ref[...] += part

        @pl.when(j == last)
        def _():
            o_ref[...] = acc_ref[...] + part

    @pl.when(jnp.logical_not(live) & (j == last))
    def _():
        o_ref[...] = jnp.zeros(o_ref.shape, o_ref.dtype)


def _moe_experts(xbuf, blk_e, nblk, w_e_in, w_e_out, *, layer, blk):
    rows, d = xbuf.shape
    n_blocks = rows // blk
    d_ff = w_e_out.shape[2]
    splits = 2
    width = d_ff // splits

    def wj(i, j, nb):
        return jnp.where(i < nb[0], j, splits - 1)

    return pl.pallas_call(
        functools.partial(_moe_kernel, width=width),
        grid_spec=pltpu.PrefetchScalarGridSpec(
            num_scalar_prefetch=2,
            grid=(n_blocks, splits),
            in_specs=[
                pl.BlockSpec((blk, d), lambda i, j, e, nb: (i, 0)),
                pl.BlockSpec((None, None, d, width), lambda i, j, e, nb: (layer, e[i], 0, wj(i, j, nb))),
                pl.BlockSpec((None, None, d, width),
                             lambda i, j, e, nb: (layer, e[i], 0, splits + wj(i, j, nb))),
                pl.BlockSpec((None, None, width, d), lambda i, j, e, nb: (layer, e[i], wj(i, j, nb), 0)),
            ],
            out_specs=pl.BlockSpec((blk, d), lambda i, j, e, nb: (i, 0)),
            scratch_shapes=[pltpu.VMEM((blk, d), _F32)],
        ),
        out_shape=jax.ShapeDtypeStruct((rows, d), _F32),
        compiler_params=_cparams("arbitrary", "arbitrary"),
        name="moe_experts",
    )(blk_e, nblk, xbuf, w_e_in, w_e_in, w_e_out)


ROUTE_LANES = 128


def _router_kernel(x_ref, wh_ref, wl_ref, br_ref, info_ref, cnt_ref, base_ref):
    @pl.when(pl.program_id(0) == 0)
    def _():
        base_ref[...] = jnp.zeros(base_ref.shape, _F32)

    x = x_ref[...]
    tm = x.shape[0]
    xh = x.astype(_BF16)
    xl = (x - xh.astype(_F32)).astype(_BF16)
    logits = _dot(xh, wh_ref[...]) + (_dot(xh, wl_ref[...]) + _dot(xl, wh_ref[...])) + br_ref[...]
    lane = lax.broadcasted_iota(jnp.int32, logits.shape, 1)
    neg = -jnp.inf
    logits = jnp.where(lane < N_EXPERTS, logits, neg)
    m1 = jnp.max(logits, axis=-1, keepdims=True)
    i1 = jnp.min(jnp.where(logits == m1, lane, ROUTE_LANES), axis=-1, keepdims=True)
    rest = jnp.where(lane == i1, neg, logits)
    m2 = jnp.max(rest, axis=-1, keepdims=True)
    i2 = jnp.min(jnp.where(rest == m2, lane, ROUTE_LANES), axis=-1, keepdims=True)
    ex = jnp.exp(m2 - m1)
    g1 = 1.0 / (1.0 + ex)
    g2 = ex / (1.0 + ex)
    oh1 = (lane == i1).astype(_F32)
    oh2 = (lane == i2).astype(_F32)
    oh = oh1 + oh2
    tri = (lax.broadcasted_iota(jnp.int32, (tm, tm), 0) > lax.broadcasted_iota(jnp.int32, (tm, tm), 1))
    before = _dot(tri.astype(_BF16), oh.astype(_BF16)) + base_ref[...]
    r1 = jnp.sum(before * oh1, axis=-1, keepdims=True)
    r2 = jnp.sum(before * oh2, axis=-1, keepdims=True)
    base_ref[...] += jnp.sum(oh, axis=0, keepdims=True)
    cnt_ref[...] = base_ref[...]
    cols = (i1.astype(_F32), i2.astype(_F32), r1, r2, g1, g2)
    info = jnp.zeros(logits.shape, _F32)
    for c, val in enumerate(cols):
        info = jnp.where(lane == c, val, info)
    info_ref[...] = info


def _router(x2, w_r, b_r, *, tm):
    n, d = x2.shape
    wp = jnp.pad(w_r, ((0, 0), (0, ROUTE_LANES - N_EXPERTS)))
    wh = wp.astype(_BF16)
    wl = (wp - wh.astype(_F32)).astype(_BF16)
    bp = jnp.pad(b_r, (0, ROUTE_LANES - N_EXPERTS))[None]
    return pl.pallas_call(
        _router_kernel,
        grid=(n // tm,),
        in_specs=[pl.BlockSpec((tm, d), lambda i: (i, 0)), _full((d, ROUTE_LANES)), _full((d, ROUTE_LANES)),
                  _full((1, ROUTE_LANES))],
        out_specs=[pl.BlockSpec((tm, ROUTE_LANES), lambda i: (i, 0)), _full((1, ROUTE_LANES))],
        out_shape=[jax.ShapeDtypeStruct((n, ROUTE_LANES), _F32), jax.ShapeDtypeStruct((1, ROUTE_LANES), _F32)],
        scratch_shapes=[pltpu.VMEM((1, ROUTE_LANES), _F32)],
        compiler_params=_cparams("arbitrary"),
        name="moe_router",
    )(x2, wh, wl, bp)


def _route_plan(info, cnt, *, blk, n_blocks):
    e = info[:, 0:TOP_K].astype(jnp.int32)
    rank = info[:, TOP_K:2 * TOP_K].astype(jnp.int32)
    counts = cnt[0, :N_EXPERTS].astype(jnp.int32)
    padded = (counts + blk - 1) // blk * blk
    pad_end = jnp.cumsum(padded)
    pad_start = pad_end - padded
    dest = (pad_start[e] + rank).reshape(-1)
    nblk = pad_end[-1] // blk
    def owner(bounds, q):
        return jnp.minimum(jnp.sum(bounds[None, :] <= q[:, None], axis=1), N_EXPERTS - 1)

    blk_e = owner(pad_end, jnp.minimum(jnp.arange(n_blocks), nblk - 1) * blk).astype(jnp.int32)
    n_fill = n_blocks * blk - dest.shape[0]
    tail = padded - counts
    gaps = jnp.cumsum(tail)
    slot = jnp.arange(n_fill)
    ge = owner(gaps, slot)
    in_gap = pad_start[ge] + counts[ge] + slot - (gaps[ge] - tail[ge])
    fill = jnp.where(slot < gaps[-1], in_gap, pad_end[-1] + slot - gaps[-1]).astype(jnp.int32)
    return dest.astype(jnp.int32), blk_e, nblk.astype(jnp.int32)[None], fill


def _row_copy(src_ref, src_row, dst_ref, dst_row, sem):
    return pltpu.make_async_copy(src_ref.at[pl.ds(src_row, 1)], dst_ref.at[pl.ds(dst_row, 1)], sem)


def _dispatch_kernel(fill_ref, dest_ref, x_ref, xbuf_ref, zero_ref, sem, zsem, *, n_fill):
    tm = x_ref.shape[0]
    zrows = zero_ref.shape[0]

    def issue(r, c):
        for k in range(TOP_K):
            _row_copy(x_ref, r, xbuf_ref, dest_ref[TOP_K * r + k], sem).start()
        return c

    lax.fori_loop(0, tm, issue, 0, unroll=8)

    @pl.when(pl.program_id(0) == pl.num_programs(0) - 1)
    def _():
        zero_ref[...] = jnp.zeros(zero_ref.shape, zero_ref.dtype)

        def zissue(r, c):
            _row_copy(zero_ref, 0, xbuf_ref, fill_ref[r], zsem).start()
            return c

        lax.fori_loop(0, n_fill, zissue, 0, unroll=8)

        def zwait(r, c):
            pltpu.make_async_copy(zero_ref, xbuf_ref.at[pl.ds(0, zrows)], zsem).wait()
            return c

        lax.fori_loop(0, n_fill // zrows, zwait, 0)

    for k in range(TOP_K):
        pltpu.make_async_copy(x_ref, xbuf_ref.at[pl.ds(0, tm)], sem).wait()


def _dispatch(x2, dest, fill, *, rows, tm):
    n, d = x2.shape
    n_fill = fill.shape[0]
    zrows = 8
    assert n_fill % zrows == 0
    return pl.pallas_call(
        functools.partial(_dispatch_kernel, n_fill=n_fill),
        grid_spec=pltpu.PrefetchScalarGridSpec(
            num_scalar_prefetch=1,
            grid=(n // tm,),
            in_specs=[pl.BlockSpec((TOP_K * tm,), lambda i, f: (i,), memory_space=pltpu.SMEM),
                      pl.BlockSpec((tm, d), lambda i, f: (i, 0))],
            out_specs=pl.BlockSpec(memory_space=pl.ANY),
            scratch_shapes=[pltpu.VMEM((zrows, d), _F32), pltpu.SemaphoreType.DMA, pltpu.SemaphoreType.DMA],
        ),
        out_shape=jax.ShapeDtypeStruct((rows, d), _F32),
        compiler_params=_cparams("arbitrary"),
        name="moe_dispatch",
    )(fill, dest, x2)


def _combine_kernel(dest_ref, next_ref, x_ref, info_ref, ybuf_ref, g_ref, b_ref, o_ref, y_ref, sems):
    i = pl.program_id(0)
    tm = x_ref.shape[0]
    slot = i % 2
    nslot = 1 - slot

    def gather_tile(idx_ref, to_slot):
        def issue(r, c):
            for k in range(TOP_K):
                _row_copy(ybuf_ref, idx_ref[TOP_K * r + k], y_ref.at[to_slot, k], r, sems.at[to_slot, k]).start()
            return c

        lax.fori_loop(0, tm, issue, 0, unroll=8)

    def wait_tile(in_slot):
        for k in range(TOP_K):
            pltpu.make_async_copy(ybuf_ref.at[pl.ds(0, tm)], y_ref.at[in_slot, k], sems.at[in_slot, k]).wait()

    @pl.when(i == 0)
    def _():
        gather_tile(dest_ref, slot)

    gather_tile(next_ref, nslot)
    wait_tile(slot)
    info = info_ref[...]
    f = y_ref[slot, 0] * info[:, 2 * TOP_K:2 * TOP_K + 1]
    for k in range(1, TOP_K):
        f = f + y_ref[slot, k] * info[:, 2 * TOP_K + k:2 * TOP_K + k + 1]
    o_ref[...] = _layer_norm(ALPHA * x_ref[...] + f, g_ref[...], b_ref[...])

    @pl.when(i == pl.num_programs(0) - 1)
    def _():
        wait_tile(nslot)


def _combine(x2, info, dest, ybuf, g, b, *, tm):
    n, d = x2.shape
    n_tiles = n // tm
    row = pl.BlockSpec((tm, d), lambda i: (i, 0))
    idx = lambda f: pl.BlockSpec((TOP_K * tm,), f, memory_space=pltpu.SMEM)
    return pl.pallas_call(
        _combine_kernel,
        grid=(n_tiles,),
        in_specs=[idx(lambda i: (i,)), idx(lambda i: (jnp.minimum(i + 1, n_tiles - 1),)), row,
                  pl.BlockSpec((tm, ROUTE_LANES), lambda i: (i, 0)), pl.BlockSpec(memory_space=pl.ANY),
                  _full((1, d)), _full((1, d))],
        out_specs=row,
        out_shape=jax.ShapeDtypeStruct((n, d), _F32),
        scratch_shapes=[pltpu.VMEM((2, TOP_K, tm, d), _F32), pltpu.SemaphoreType.DMA((2, TOP_K))],
        compiler_params=_cparams("arbitrary"),
        name="moe_combine",
    )(dest, dest, x2, info, ybuf, g[None], b[None])


def _moe(x2, w_r, b_r, w_e_in, w_e_out, g, b, *, layer, tm):
    n, d = x2.shape
    n_assign = n * TOP_K
    blk = min(MOE_BLOCK, max(8, -(-n_assign // (8 * N_EXPERTS)) * 8))
    n_blocks = -(-(n_assign + N_EXPERTS * (blk - 1)) // blk)
    info, cnt = _router(x2, w_r, b_r, tm=tm)
    dest, blk_e, nblk, fill = _route_plan(info, cnt, blk=blk, n_blocks=n_blocks)
    xbuf = _dispatch(x2, dest, fill, rows=n_blocks * blk, tm=tm)
    ybuf = _moe_experts(xbuf, blk_e, nblk, w_e_in, w_e_out, layer=layer, blk=blk)
    return _combine(x2, info, dest, ybuf, g, b, tm=tm)


def kernel(x_prompt, x_sample, cache_b_k, cache_b_v, state_c_conv, ln_g, ln_b, w_a_in, b_a_in, ln_a_g, ln_a_b,
           w_a_s, b_a_s, w_a_out, w_b_qkv, b_b_qkv, b_sink, w_b_out, w_c_in, b_c_in, w_c_dw, b_c_dw, ln_c_g,
           ln_c_b, w_c_out, w_f_in, w_f_out, w_r, b_r, w_e_in, w_e_out):
    bsz, t, d = x_prompt.shape
    dbs, s, _ = x_sample.shape
    n_p, n_s = bsz * t, dbs * s
    xp = x_prompt.reshape(n_p, d)
    xs = x_sample.reshape(n_s, d)
    a_v_s, b_k_p, b_v_p, b_k_s, b_v_s, c_p, c_s = [], [], [], [], [], [], []
    w_f_in_b, w_f_out_b = w_f_in.astype(_BF16), w_f_out.astype(_BF16)
    w_e_in_b, w_e_out_b = w_e_in.astype(_BF16), w_e_out.astype(_BF16)
    for i in range(DEPTH):
        kind, j = i % N_MIXERS, i // N_MIXERS
        g0, b0 = ln_g[i, 0], ln_b[i, 0]
        if kind == 0:
            args = (w_a_in[j], b_a_in[j], ln_a_g[j], ln_a_b[j], w_a_s[j], b_a_s[j], w_a_out[j], g0, b0)
            (xp,) = _gmlp(xp, *args, chunk=A_CHUNK, tm=ROW_TILE, emit_v=False)
            xs, v_rows = _gmlp(xs, *args, chunk=s, tm=n_s, emit_v=True)
            a_v_s.append(v_rows.reshape(dbs, s, d))
        elif kind == 1:
            args = (w_b_qkv[j], b_b_qkv[j], b_sink[j], w_b_out[j], g0, b0)
            xp3, kp, vp = _swa_prompt(xp.reshape(bsz, t, d), *args, tm=ROW_TILE)
            xs3, kn, vn = _swa_sample(xs.reshape(dbs, s, d), cache_b_k[j], cache_b_v[j], *args)
            xp, xs = xp3.reshape(n_p, d), xs3.reshape(n_s, d)
            b_k_p.append(kp)
            b_v_p.append(vp)
            b_k_s.append(kn)
            b_v_s.append(vn)
        else:
            args = (w_c_in[j], b_c_in[j], w_c_dw[j], b_c_dw[j], ln_c_g[j], ln_c_b[j], w_c_out[j], g0, b0)
            zero_state = jnp.zeros((bsz, CONV_WIDTH - 1, d), _F32)
            xp3, cp = _conv(xp.reshape(bsz, t, d), zero_state, *args, tm=ROW_TILE)
            xs3, cs = _conv(xs.reshape(dbs, s, d), state_c_conv[j], *args, tm=s)
            xp, xs = xp3.reshape(n_p, d), xs3.reshape(n_s, d)
            c_p.append(cp)
            c_s.append(cs)
        f = i // 2
        g1, b1 = ln_g[i, 1], ln_b[i, 1]
        if i % 2 == 0:
            xp = _ffn(xp, w_f_in_b, w_f_out_b, g1, b1, layer=f, tm=ROW_TILE)
            xs = _ffn(xs, w_f_in_b, w_f_out_b, g1, b1, layer=f, tm=n_s)
        else:
            xp = _moe(xp, w_r[f], b_r[f], w_e_in_b, w_e_out_b, g1, b1, layer=f, tm=ROW_TILE)
            xs = _moe(xs, w_r[f], b_r[f], w_e_in_b, w_e_out_b, g1, b1, layer=f, tm=n_s)
    return (xp.reshape(bsz, t, d), xs.reshape(dbs, s, d), jnp.stack(a_v_s), jnp.stack(b_k_p),
            jnp.stack(b_v_p), jnp.stack(b_k_s), jnp.stack(b_v_s), jnp.stack(c_p), jnp.stack(c_s))
```

```python
import functools

import jax
import jax.numpy as jnp
import numpy as np
from jax import lax
from jax.experimental import pallas as pl
from jax.experimental.pallas import tpu as pltpu

D_MODEL = 1024
DEPTH = 4
PAST_LEN = 1024
CHUNK = 64
N_MIXERS = 3
ALPHA = (2 * DEPTH) ** 0.25
LN_EPS = 1e-5
A_CHUNK = 128
A_GROUPS = 4
HEAD_DIM = 64
B_Q_HEADS = D_MODEL // HEAD_DIM
B_KV_HEADS = 4
B_GROUP = B_Q_HEADS // B_KV_HEADS
WINDOW = 128
BAND_CHUNKS = WINDOW // CHUNK
ROT_DIM = HEAD_DIM // 4
ROPE_THETA = 500000.0
CONV_WIDTH = 31
N_EXPERTS = 8
TOP_K = 2
MOE_BLOCK = 512

VMEM_LIMIT_BYTES = 56 * 1024 * 1024
ROW_TILE = 512
FFN_ROW_TILE = 1024
FF_SUB = 256
CONV_HALO = 32
GROUP_LANES = B_GROUP * HEAD_DIM
ROPE_LANES = 128

_BF16 = jnp.bfloat16
_F32 = jnp.float32


def _cparams(*sem):
    return pltpu.CompilerParams(dimension_semantics=sem, vmem_limit_bytes=VMEM_LIMIT_BYTES)


def _dot(a, b):
    return jnp.dot(a, b, preferred_element_type=_F32)


def _layer_norm(x, g, b):
    mu = jnp.mean(x, axis=-1, keepdims=True)
    xc = x - mu
    var = jnp.mean(xc * xc, axis=-1, keepdims=True)
    return xc * lax.rsqrt(var + LN_EPS) * g + b


def _sigmoid(x):
    return 1.0 / (1.0 + jnp.exp(-x))


def _full(shape):
    nd = len(shape)
    return pl.BlockSpec(shape, lambda *_: (0,) * nd)


def _gmlp_kernel(x_ref, w_in_ref, b_in_ref, g_a_ref, b_a_ref, w_s_ref, bz_ref, w_out_ref,
                 g_ref, b_ref, *out_refs, chunk, emit_v):
    o_ref = out_refs[0]
    x = x_ref[...]
    tm = x.shape[0]
    h = _dot(x.astype(_BF16), w_in_ref[...]) + b_in_ref[...]
    h = jax.nn.gelu(h)
    u = h[:, :D_MODEL]
    v = _layer_norm(h[:, D_MODEL:], g_a_ref[...], b_a_ref[...])
    if emit_v:
        out_refs[1][...] = v
    vb = v.astype(_BF16)
    ri = lax.broadcasted_iota(jnp.int32, (chunk, chunk), 0) // CHUNK
    ci = lax.broadcasted_iota(jnp.int32, (chunk, chunk), 1) // CHUNK
    gw = D_MODEL // A_GROUPS
    ws = [jnp.where(ci <= ri, w_s_ref[g], 0.0).astype(_BF16) for g in range(A_GROUPS)]
    rows = []
    for n in range(tm // chunk):
        r0 = n * chunk
        cols = [_dot(ws[g], vb[r0:r0 + chunk, g * gw:(g + 1) * gw]) for g in range(A_GROUPS)]
        rows.append(jnp.concatenate(cols, axis=1) + bz_ref[...])
    z = jnp.concatenate(rows, axis=0) if len(rows) > 1 else rows[0]
    y = _dot((u * z).astype(_BF16), w_out_ref[...])
    o_ref[...] = _layer_norm(ALPHA * x + y, g_ref[...], b_ref[...])


def _gmlp(x2, w_in, b_in, g_a, b_a, w_s, b_s, w_out, g, b, *, chunk, tm, emit_v):
    n = x2.shape[0]
    d = D_MODEL
    gw = d // A_GROUPS
    bz = jnp.repeat(b_s[:, :chunk].T, gw, axis=1)
    out_shape = [jax.ShapeDtypeStruct((n, d), _F32)]
    out_specs = [pl.BlockSpec((tm, d), lambda i: (i, 0))]
    if emit_v:
        out_shape.append(jax.ShapeDtypeStruct((n, d), _F32))
        out_specs.append(pl.BlockSpec((tm, d), lambda i: (i, 0)))
    res = pl.pallas_call(
        functools.partial(_gmlp_kernel, chunk=chunk, emit_v=emit_v),
        grid=(n // tm,),
        in_specs=[
            pl.BlockSpec((tm, d), lambda i: (i, 0)),
            _full((d, 2 * d)), _full((1, 2 * d)), _full((1, d)), _full((1, d)),
            _full((A_GROUPS, chunk, chunk)), _full((chunk, d)), _full((d, d)),
            _full((1, d)), _full((1, d)),
        ],
        out_specs=out_specs,
        out_shape=out_shape,
        compiler_params=_cparams("arbitrary"),
        name="gmlp_mixer",
    )(x2, w_in.astype(_BF16), b_in[None], g_a[None], b_a[None], w_s[:, :chunk, :chunk], bz,
      w_out.astype(_BF16), g[None], b[None])
    return res


def _rope_tables(pos):
    half = ROT_DIM // 2
    inv = ROPE_THETA ** (-np.arange(0, ROT_DIM, 2, dtype=np.float32) / ROT_DIM)
    lane = np.arange(ROPE_LANES) % HEAD_DIM
    inv_lane = jnp.asarray(inv[lane % half], _F32)[None, :]
    lo = jnp.asarray(lane < half)[None, :]
    hi = jnp.asarray((lane >= half) & (lane < ROT_DIM))[None, :]
    ang = pos.astype(_F32)[:, None] * inv_lane
    cos, sin = jnp.cos(ang), jnp.sin(ang)
    c = jnp.where(lo | hi, cos, 1.0)
    s_up = jnp.where(hi, sin, 0.0)
    s_dn = jnp.where(lo, -sin, 0.0)
    return c, s_up, s_dn


def _rotary(x, c, s_up, s_dn):
    half = ROT_DIM // 2
    slabs = []
    for i in range(x.shape[1] // ROPE_LANES):
        xs = x[:, i * ROPE_LANES:(i + 1) * ROPE_LANES]
        slabs.append(xs * c + pltpu.roll(xs, half, axis=1) * s_up
                     + pltpu.roll(xs, ROPE_LANES - half, axis=1) * s_dn)
    return jnp.concatenate(slabs, axis=1)


def _expand_kv_weights(w_qkv, b_qkv):
    nq = B_Q_HEADS * HEAD_DIM
    nk = B_KV_HEADS * HEAD_DIM
    scale = HEAD_DIM ** -0.5
    assert float(np.log2(scale)).is_integer()

    def rep(a):
        lead = a.shape[:-1]
        a = a.reshape(lead + (B_KV_HEADS, 1, HEAD_DIM))
        a = jnp.broadcast_to(a, lead + (B_KV_HEADS, B_GROUP, HEAD_DIM))
        return a.reshape(lead + (nq,))

    w = jnp.concatenate([w_qkv[:, :nq] * scale, rep(w_qkv[:, nq:nq + nk]), rep(w_qkv[:, nq + nk:])], axis=1)
    bb = jnp.concatenate([b_qkv[:nq] * scale, rep(b_qkv[nq:nq + nk]), rep(b_qkv[nq + nk:])])
    return w, bb


def _attend(q, kwin, vwin, valid, sink_col):
    nq = q.shape[0]
    lane_head = lax.broadcasted_iota(jnp.int32, (1, GROUP_LANES), 1) // HEAD_DIM
    masks = [(lane_head == g).astype(_F32) for g in range(B_GROUP)]
    qs = jnp.concatenate([q * masks[g] for g in range(B_GROUP)], axis=0).astype(_BF16)
    s = lax.dot_general(qs, kwin, (((1,), (1,)), ((), ())), preferred_element_type=_F32)
    if valid is not None:
        s = jnp.where(valid, s, -1e30)
    m = jnp.maximum(jnp.max(s, axis=-1, keepdims=True), sink_col)
    p = jnp.exp(s - m)
    p = p / (jnp.sum(p, axis=-1, keepdims=True) + jnp.exp(sink_col - m))
    o = _dot(p.astype(_BF16), vwin)
    out = o[0:nq] * masks[0]
    for g in range(1, B_GROUP):
        out = out + o[g * nq:(g + 1) * nq] * masks[g]
    return out


def _sink_cols(sink_ref, hk, nq):
    return jnp.concatenate(
        [jnp.full((nq, 1), sink_ref[hk * B_GROUP + g], _F32) for g in range(B_GROUP)], axis=0)


SCORE_LANES = 256
SOFTMAX_ROWS = 512


def _swa_prompt_kernel(sink_ref, x_ref, w_ref, bq_ref, c_ref, su_ref, sd_ref, w_out_ref, g_ref, b_ref,
                       o_ref, kc_ref, vc_ref, k_ext, v_ext, s_ref, p_ref):
    t = pl.program_id(1)
    tm = x_ref.shape[0]
    d = D_MODEL
    halo = BAND_CHUNKS * CHUNK
    nkeys = (BAND_CHUNKS + 1) * CHUNK
    n_chunks = tm // CHUNK
    grp = B_GROUP * CHUNK

    @pl.when(t == 0)
    def _():
        k_ext[0:halo, :] = jnp.zeros((halo, d), _BF16)
        v_ext[0:halo, :] = jnp.zeros((halo, d), _BF16)
        k_ext[halo + tm:, :] = jnp.zeros((k_ext.shape[0] - halo - tm, d), _BF16)

    x = x_ref[...]
    qkv = _dot(x.astype(_BF16), w_ref[...]) + bq_ref[...]
    qk = _rotary(qkv[:, :2 * d], c_ref[...], su_ref[...], sd_ref[...])
    q = qk[:, :d]
    k = qk[:, d:]
    v = qkv[:, 2 * d:]
    k_ext[halo:halo + tm, :] = k.astype(_BF16)
    v_ext[halo:halo + tm, :] = v.astype(_BF16)
    kc_ref[...] = k[tm - WINDOW:, :]
    vc_ref[...] = v[tm - WINDOW:, :]

    lane_head = lax.broadcasted_iota(jnp.int32, (1, GROUP_LANES), 1) // HEAD_DIM
    masks = [(lane_head == g).astype(_F32) for g in range(B_GROUP)]
    key_lane = lax.broadcasted_iota(jnp.int32, (grp, SCORE_LANES), 1)
    tails = [jnp.where(key_lane == nkeys, _sink_cols(sink_ref, hk, CHUNK), -1e30) for hk in range(B_KV_HEADS)]

    for n in range(n_chunks):
        r0 = n * CHUNK
        if n < BAND_CHUNKS:
            first = jnp.maximum((BAND_CHUNKS - n - t * n_chunks) * CHUNK, 0)
            keep = (lax.bitcast_convert_type(key_lane - first, jnp.uint32)
                    < lax.bitcast_convert_type(nkeys - first, jnp.uint32))
        else:
            keep = key_lane < nkeys
        for hk in range(B_KV_HEADS):
            l0 = hk * GROUP_LANES
            qn = q[r0:r0 + CHUNK, l0:l0 + GROUP_LANES]
            qs = jnp.concatenate([qn * masks[g] for g in range(B_GROUP)], axis=0).astype(_BF16)
            s = lax.dot_general(qs, k_ext[r0:r0 + SCORE_LANES, l0:l0 + GROUP_LANES],
                                (((1,), (1,)), ((), ())), preferred_element_type=_F32)
            i0 = (n * B_KV_HEADS + hk) * grp
            s_ref[i0:i0 + grp, :] = jnp.where(keep, s, tails[hk])

    def softmax_rows(i, c):
        rows = pl.ds(pl.multiple_of(i * SOFTMAX_ROWS, SOFTMAX_ROWS), SOFTMAX_ROWS)
        s = s_ref[rows, :]
        p = jnp.exp(s - jnp.max(s, axis=-1, keepdims=True))
        p_ref[rows, :] = (p / jnp.sum(p, axis=-1, keepdims=True)).astype(_BF16)
        return c

    lax.fori_loop(0, s_ref.shape[0] // SOFTMAX_ROWS, softmax_rows, 0)

    rows = []
    for n in range(n_chunks):
        r0 = n * CHUNK
        cols = []
        for hk in range(B_KV_HEADS):
            l0 = hk * GROUP_LANES
            i0 = (n * B_KV_HEADS + hk) * grp
            o = _dot(p_ref[i0:i0 + grp, 0:nkeys], v_ext[r0:r0 + nkeys, l0:l0 + GROUP_LANES])
            out = o[0:CHUNK] * masks[0]
            for g in range(1, B_GROUP):
                out = out + o[g * CHUNK:(g + 1) * CHUNK] * masks[g]
            cols.append(out)
        rows.append(jnp.concatenate(cols, axis=1))
    att = jnp.concatenate(rows, axis=0)
    y = _dot(att.astype(_BF16), w_out_ref[...])
    o_ref[...] = _layer_norm(ALPHA * x + y, g_ref[...], b_ref[...])
    k_ext[0:halo, :] = k_ext[tm:tm + halo, :]
    v_ext[0:halo, :] = v_ext[tm:tm + halo, :]


def _unexpand(a):
    lead = a.shape[:-1]
    return a.reshape(lead + (B_KV_HEADS, B_GROUP, HEAD_DIM))[..., 0, :]


def _swa_prompt(x3, w_qkv, b_qkv, sink, w_out, g, b, *, tm):
    bsz, t, d = x3.shape
    w_exp, b_exp = _expand_kv_weights(w_qkv, b_qkv)
    c, su, sd = _rope_tables(jnp.arange(t))
    tab = pl.BlockSpec((tm, 2 * HEAD_DIM), lambda bi, ti, *_: (ti, 0))
    xspec = pl.BlockSpec((None, tm, d), lambda bi, ti, *_: (bi, ti, 0))
    cspec = pl.BlockSpec((None, WINDOW, d), lambda bi, ti, *_: (bi, 0, 0))
    halo = BAND_CHUNKS * CHUNK
    score_rows = (tm // CHUNK) * B_KV_HEADS * B_GROUP * CHUNK
    assert score_rows % SOFTMAX_ROWS == 0
    one = pl.Buffered(1)
    y, kc, vc = pl.pallas_call(
        _swa_prompt_kernel,
        grid_spec=pltpu.PrefetchScalarGridSpec(
            num_scalar_prefetch=1,
            grid=(bsz, t // tm),
            in_specs=[xspec, pl.BlockSpec((d, 3 * d), lambda *_: (0, 0), pipeline_mode=one),
                      _full((1, 3 * d)), tab, tab, tab,
                      pl.BlockSpec((d, d), lambda *_: (0, 0), pipeline_mode=one),
                      _full((1, d)), _full((1, d))],
            out_specs=[xspec, cspec, cspec],
            scratch_shapes=[
                pltpu.VMEM((halo + tm + SCORE_LANES - (halo + CHUNK), d), _BF16),
                pltpu.VMEM((halo + tm, d), _BF16),
                pltpu.VMEM((score_rows, SCORE_LANES), _F32),
                pltpu.VMEM((score_rows, SCORE_LANES), _BF16)],
        ),
        out_shape=[jax.ShapeDtypeStruct((bsz, t, d), _F32),
                   jax.ShapeDtypeStruct((bsz, WINDOW, d), _F32),
                   jax.ShapeDtypeStruct((bsz, WINDOW, d), _F32)],
        compiler_params=_cparams("arbitrary", "arbitrary"),
        name="swa_prompt_mixer",
    )(sink, x3, w_exp.astype(_BF16), b_exp[None], c, su, sd, w_out.astype(_BF16), g[None], b[None])
    return y, _unexpand(kc), _unexpand(vc)


def _swa_sample_kernel(sink_ref, x_ref, ck_ref, cv_ref, valid_ref, w_ref, bq_ref, c_ref, su_ref, sd_ref,
                       w_out_ref, g_ref, b_ref, o_ref, kn_ref, vn_ref, *, bsz, s):
    d = D_MODEL
    x = x_ref[...]
    qkv = _dot(x.astype(_BF16), w_ref[...]) + bq_ref[...]
    qk = _rotary(qkv[:, :2 * d], c_ref[...], su_ref[...], sd_ref[...])
    q = qk[:, :d]
    k = qk[:, d:]
    v = qkv[:, 2 * d:]
    kn_ref[...] = k
    vn_ref[...] = v
    kb = k.astype(_BF16)
    vb = v.astype(_BF16)
    valid = jnp.concatenate([valid_ref[...] > 0] * B_GROUP, axis=0)
    rows = []
    for bi in range(bsz):
        r0 = bi * s
        kk = jnp.concatenate([ck_ref[bi], kb[r0:r0 + s]], axis=0)
        vv = jnp.concatenate([cv_ref[bi], vb[r0:r0 + s]], axis=0)
        cols = []
        for hk in range(B_KV_HEADS):
            l0 = hk * GROUP_LANES
            cols.append(_attend(q[r0:r0 + s, l0:l0 + GROUP_LANES], kk[:, l0:l0 + GROUP_LANES],
                                vv[:, l0:l0 + GROUP_LANES], valid, _sink_cols(sink_ref, hk, s)))
        rows.append(jnp.concatenate(cols, axis=1))
    att = jnp.concatenate(rows, axis=0)
    y = _dot(att.astype(_BF16), w_out_ref[...])
    o_ref[...] = _layer_norm(ALPHA * x + y, g_ref[...], b_ref[...])


def _swa_sample(x3, ck, cv, w_qkv, b_qkv, sink, w_out, g, b):
    bsz, s, d = x3.shape
    nc = ck.shape[1]
    w_exp, b_exp = _expand_kv_weights(w_qkv, b_qkv)
    qpos = PAST_LEN + np.arange(s)
    kpos = np.concatenate([PAST_LEN - nc + np.arange(nc), qpos])
    qc, kc = qpos // CHUNK, kpos // CHUNK
    valid = ((kc[None, :] <= qc[:, None]) & (kc[None, :] >= qc[:, None] - BAND_CHUNKS)).astype(np.int32)
    c, su, sd = _rope_tables(jnp.tile(jnp.asarray(qpos), bsz))

    def expand_cache(a):
        a = jnp.broadcast_to(a[:, :, :, None, :], (bsz, nc, B_KV_HEADS, B_GROUP, HEAD_DIM))
        return a.reshape(bsz, nc, d).astype(_BF16)

    n = bsz * s
    vm = pl.BlockSpec(memory_space=pltpu.VMEM)
    y, kn, vn = pl.pallas_call(
        functools.partial(_swa_sample_kernel, bsz=bsz, s=s),
        in_specs=[pl.BlockSpec(memory_space=pltpu.SMEM)] + [vm] * 12,
        out_specs=[vm, vm, vm],
        out_shape=[jax.ShapeDtypeStruct((n, d), _F32)] * 3,
        compiler_params=pltpu.CompilerParams(vmem_limit_bytes=VMEM_LIMIT_BYTES),
        name="swa_sample_mixer",
    )(sink, x3.reshape(n, d), expand_cache(ck), expand_cache(cv), jnp.asarray(valid),
      w_exp.astype(_BF16), b_exp[None], c, su, sd, w_out.astype(_BF16), g[None], b[None])
    return (y.reshape(bsz, s, d), _unexpand(kn.reshape(bsz, s, d)), _unexpand(vn.reshape(bsz, s, d)))


def _conv_kernel(x_ref, st_ref, w_in_ref, b_in_ref, w_dw_ref, b_dw_ref, g_c_ref, b_c_ref, w_out_ref,
                 g_ref, b_ref, o_ref, st_out_ref, h_ext):
    t = pl.program_id(1)
    tm = x_ref.shape[0]
    d = D_MODEL

    @pl.when(t == 0)
    def _():
        h_ext[0:CONV_HALO, :] = st_ref[...]

    x = x_ref[...]
    ag = _dot(x.astype(_BF16), w_in_ref[...]) + b_in_ref[...]
    h = ag[:, :d] * _sigmoid(ag[:, d:])
    h_ext[CONV_HALO:CONV_HALO + tm, :] = h
    base = CONV_HALO - (CONV_WIDTH - 1)
    acc = None
    for sub in range(8):
        offs = [o for o in range(base, base + CONV_WIDTH) if o % 8 == sub]
        if not offs:
            continue
        rows = tm + (8 if sub else 0)
        part = None
        for o in offs:
            a0 = o - sub
            term = h_ext[a0:a0 + rows, :] * w_dw_ref[o - base:o - base + 1, :]
            part = term if part is None else part + term
        part = part[sub:sub + tm, :]
        acc = part if acc is None else acc + part
    y = acc + b_dw_ref[...]
    y = _layer_norm(y, g_c_ref[...], b_c_ref[...])
    y = y * _sigmoid(y)
    y = _dot(y.astype(_BF16), w_out_ref[...])
    o_ref[...] = _layer_norm(ALPHA * x + y, g_ref[...], b_ref[...])
    new_hist = h_ext[tm:tm + CONV_HALO, :]
    st_out_ref[...] = new_hist
    h_ext[0:CONV_HALO, :] = new_hist


def _conv(x3, state, w_in, b_in, w_dw, b_dw, g_c, b_c, w_out, g, b, *, tm):
    bsz, t, d = x3.shape
    st = jnp.pad(state, ((0, 0), (CONV_HALO - (CONV_WIDTH - 1), 0), (0, 0)))
    xspec = pl.BlockSpec((None, tm, d), lambda bi, ti: (bi, ti, 0))
    sspec = pl.BlockSpec((None, CONV_HALO, d), lambda bi, ti: (bi, 0, 0))
    y, st_new = pl.pallas_call(
        _conv_kernel,
        grid=(bsz, t // tm),
        in_specs=[xspec, sspec, _full((d, 2 * d)), _full((1, 2 * d)), _full((CONV_WIDTH, d)), _full((1, d)),
                  _full((1, d)), _full((1, d)), _full((d, d)), _full((1, d)), _full((1, d))],
        out_specs=[xspec, sspec],
        out_shape=[jax.ShapeDtypeStruct((bsz, t, d), _F32),
                   jax.ShapeDtypeStruct((bsz, CONV_HALO, d), _F32)],
        scratch_shapes=[pltpu.VMEM((tm + CONV_HALO, d), _F32)],
        compiler_params=_cparams("arbitrary", "arbitrary"),
        name="conv_mixer",
    )(x3, st, w_in.astype(_BF16), b_in[None], w_dw, b_dw[None], g_c[None], b_c[None],
      w_out.astype(_BF16), g[None], b[None])
    return y, st_new[:, CONV_HALO - (CONV_WIDTH - 1):]


def _swiglu_partial(xb, wg_ref, wu_ref, wo_ref, width):
    acc = None
    for c in range(width // FF_SUB):
        sl = slice(c * FF_SUB, (c + 1) * FF_SUB)
        gt = _dot(xb, wg_ref[:, sl])
        up = _dot(xb, wu_ref[:, sl])
        hh = (gt * _sigmoid(gt) * up).astype(_BF16)
        part = _dot(hh, wo_ref[sl, :])
        acc = part if acc is None else acc + part
    return acc


def _ffn_kernel(x_ref, wg_ref, wu_ref, wo_ref, g_ref, b_ref, o_ref, *, d_ff):
    x = x_ref[...]
    f = _swiglu_partial(x.astype(_BF16), wg_ref, wu_ref, wo_ref, d_ff)
    o_ref[...] = _layer_norm(ALPHA * x + f, g_ref[...], b_ref[...])


def _ffn(x2, w_in, w_out, g, b, *, layer, tm):
    n, d = x2.shape
    d_ff = w_out.shape[1]
    one = pl.Buffered(1)
    return pl.pallas_call(
        functools.partial(_ffn_kernel, d_ff=d_ff),
        grid=(n // tm,),
        in_specs=[
            pl.BlockSpec((tm, d), lambda i: (i, 0)),
            pl.BlockSpec((None, d, d_ff), lambda i: (layer, 0, 0), pipeline_mode=one),
            pl.BlockSpec((None, d, d_ff), lambda i: (layer, 0, 1), pipeline_mode=one),
            pl.BlockSpec((None, d_ff, d), lambda i: (layer, 0, 0), pipeline_mode=one),
            _full((1, d)), _full((1, d)),
        ],
        out_specs=pl.BlockSpec((tm, d), lambda i: (i, 0)),
        out_shape=jax.ShapeDtypeStruct((n, d), _F32),
        compiler_params=_cparams("arbitrary"),
        name="dense_swiglu",
    )(x2, w_in, w_in, w_out, g[None], b[None])


def _moe_kernel(blk_e_ref, nblk_ref, x_ref, wg_ref, wu_ref, wo_ref, o_ref, acc_ref, *, width):
    bi = pl.program_id(0)
    j = pl.program_id(1)
    last = pl.num_programs(1) - 1
    live = bi < nblk_ref[0]

    @pl.when(live)
    def _():
        part = _swiglu_partial(x_ref[...].astype(_BF16), wg_ref, wu_ref, wo_ref, width)

        @pl.when(j == 0)
        def _():
            acc_ref[...] = part

        @pl.when((j > 0) & (j < last))
        def _():
            acc_ref[...] += part

        @pl.when(j == last)
        def _():
            o_ref[...] = acc_ref[...] + part

    @pl.when(jnp.logical_not(live) & (j == last))
    def _():
        o_ref[...] = jnp.zeros(o_ref.shape, o_ref.dtype)


def _moe_experts(xbuf, blk_e, nblk, w_e_in, w_e_out, *, blk):
    rows, d = xbuf.shape
    n_blocks = rows // blk
    d_ff = w_e_out.shape[1]
    splits = 2
    width = d_ff // splits

    def wj(i, j, nb):
        return jnp.where(i < nb[0], j, splits - 1)

    return pl.pallas_call(
        functools.partial(_moe_kernel, width=width),
        grid_spec=pltpu.PrefetchScalarGridSpec(
            num_scalar_prefetch=2,
            grid=(n_blocks, splits),
            in_specs=[
                pl.BlockSpec((blk, d), lambda i, j, e, nb: (i, 0)),
                pl.BlockSpec((None, d, width), lambda i, j, e, nb: (e[i], 0, wj(i, j, nb))),
                pl.BlockSpec((None, d, width), lambda i, j, e, nb: (e[i], 0, splits + wj(i, j, nb))),
                pl.BlockSpec((None, width, d), lambda i, j, e, nb: (e[i], wj(i, j, nb), 0)),
            ],
            out_specs=pl.BlockSpec((blk, d), lambda i, j, e, nb: (i, 0)),
            scratch_shapes=[pltpu.VMEM((blk, d), _F32)],
        ),
        out_shape=jax.ShapeDtypeStruct((rows, d), _F32),
        compiler_params=_cparams("arbitrary", "arbitrary"),
        name="moe_experts",
    )(blk_e, nblk, xbuf, w_e_in, w_e_in, w_e_out)


ROUTE_LANES = 128


def _router_kernel(x_ref, wh_ref, wl_ref, br_ref, *rest, cast_weights):
    if cast_weights:
        w_in_ref, w_out_ref, info_ref, cnt_ref, w_in_b_ref, w_out_b_ref, base_ref = rest
        w_in_b_ref[...] = w_in_ref[...].astype(_BF16)
        w_out_b_ref[...] = w_out_ref[...].astype(_BF16)
    else:
        info_ref, cnt_ref, base_ref = rest

    @pl.when(pl.program_id(0) == 0)
    def _():
        base_ref[...] = jnp.zeros(base_ref.shape, _F32)

    x = x_ref[...]
    tm = x.shape[0]
    xh = x.astype(_BF16)
    xl = (x - xh.astype(_F32)).astype(_BF16)
    logits = _dot(xh, wh_ref[...]) + (_dot(xh, wl_ref[...]) + _dot(xl, wh_ref[...])) + br_ref[...]
    lane = lax.broadcasted_iota(jnp.int32, logits.shape, 1)
    neg = -jnp.inf
    logits = jnp.where(lane < N_EXPERTS, logits, neg)
    m1 = jnp.max(logits, axis=-1, keepdims=True)
    i1 = jnp.min(jnp.where(logits == m1, lane, ROUTE_LANES), axis=-1, keepdims=True)
    rest = jnp.where(lane == i1, neg, logits)
    m2 = jnp.max(rest, axis=-1, keepdims=True)
    i2 = jnp.min(jnp.where(rest == m2, lane, ROUTE_LANES), axis=-1, keepdims=True)
    ex = jnp.exp(m2 - m1)
    g1 = 1.0 / (1.0 + ex)
    g2 = ex / (1.0 + ex)
    oh1 = (lane == i1).astype(_F32)
    oh2 = (lane == i2).astype(_F32)
    oh = oh1 + oh2
    tri = (lax.broadcasted_iota(jnp.int32, (tm, tm), 0) > lax.broadcasted_iota(jnp.int32, (tm, tm), 1))
    before = _dot(tri.astype(_BF16), oh.astype(_BF16)) + base_ref[...]
    r1 = jnp.sum(before * oh1, axis=-1, keepdims=True)
    r2 = jnp.sum(before * oh2, axis=-1, keepdims=True)
    base_ref[...] += jnp.sum(oh, axis=0, keepdims=True)
    cnt_ref[...] = base_ref[...]
    cols = (i1.astype(_F32), i2.astype(_F32), r1, r2, g1, g2)
    info = jnp.zeros(logits.shape, _F32)
    for c, val in enumerate(cols):
        info = jnp.where(lane == c, val, info)
    info_ref[...] = info


def _router(x2, w_r, b_r, *, tm, cast=None):
    n, d = x2.shape
    n_tiles = n // tm
    wp = jnp.pad(w_r, ((0, 0), (0, ROUTE_LANES - N_EXPERTS)))
    wh = wp.astype(_BF16)
    wl = (wp - wh.astype(_F32)).astype(_BF16)
    bp = jnp.pad(b_r, (0, ROUTE_LANES - N_EXPERTS))[None]
    in_specs = [pl.BlockSpec((tm, d), lambda i: (i, 0)), _full((d, ROUTE_LANES)), _full((d, ROUTE_LANES)),
                _full((1, ROUTE_LANES))]
    out_specs = [pl.BlockSpec((tm, ROUTE_LANES), lambda i: (i, 0)), _full((1, ROUTE_LANES))]
    out_shape = [jax.ShapeDtypeStruct((n, ROUTE_LANES), _F32), jax.ShapeDtypeStruct((1, ROUTE_LANES), _F32)]
    args = [x2, wh, wl, bp]
    if cast is not None:
        w_e_in, w_e_out, layer = cast
        n_e = w_e_in.shape[1]
        per_e = n_tiles // n_e
        assert n_tiles == per_e * n_e
        for w in (w_e_in, w_e_out):
            rows, cols = w.shape[2] // per_e, w.shape[3]
            assert w.shape[2] == rows * per_e and rows % 16 == 0
            in_specs.append(pl.BlockSpec((None, None, rows, cols),
                                         lambda i: (layer, i // per_e, i % per_e, 0)))
            out_specs.append(pl.BlockSpec((None, rows, cols), lambda i: (i // per_e, i % per_e, 0)))
            out_shape.append(jax.ShapeDtypeStruct(w.shape[1:], _BF16))
            args.append(w)
    return pl.pallas_call(
        functools.partial(_router_kernel, cast_weights=cast is not None),
        grid=(n_tiles,),
        in_specs=in_specs,
        out_specs=out_specs,
        out_shape=out_shape,
        scratch_shapes=[pltpu.VMEM((1, ROUTE_LANES), _F32)],
        compiler_params=_cparams("arbitrary"),
        name="moe_router",
    )(*args)


def _route_plan(info, cnt, *, blk, n_blocks):
    e = info[:, 0:TOP_K].astype(jnp.int32)
    rank = info[:, TOP_K:2 * TOP_K].astype(jnp.int32)
    counts = cnt[0, :N_EXPERTS].astype(jnp.int32)
    padded = (counts + blk - 1) // blk * blk
    pad_end = jnp.cumsum(padded)
    pad_start = pad_end - padded
    dest = (pad_start[e] + rank).reshape(-1)
    nblk = pad_end[-1] // blk
    def owner(bounds, q):
        return jnp.minimum(jnp.sum(bounds[None, :] <= q[:, None], axis=1), N_EXPERTS - 1)

    blk_e = owner(pad_end, jnp.minimum(jnp.arange(n_blocks), nblk - 1) * blk).astype(jnp.int32)
    n_fill = n_blocks * blk - dest.shape[0]
    tail = padded - counts
    gaps = jnp.cumsum(tail)
    slot = jnp.arange(n_fill)
    ge = owner(gaps, slot)
    in_gap = pad_start[ge] + counts[ge] + slot - (gaps[ge] - tail[ge])
    fill = jnp.where(slot < gaps[-1], in_gap, pad_end[-1] + slot - gaps[-1]).astype(jnp.int32)
    return dest.astype(jnp.int32), blk_e, nblk.astype(jnp.int32)[None], fill


def _row_copy(src_ref, src_row, dst_ref, dst_row, sem):
    return pltpu.make_async_copy(src_ref.at[pl.ds(src_row, 1)], dst_ref.at[pl.ds(dst_row, 1)], sem)


def _dispatch_kernel(fill_ref, dest_ref, x_ref, xbuf_ref, zero_ref, sem, zsem, *, n_fill):
    tm = x_ref.shape[0]
    zrows = zero_ref.shape[0]

    def issue(r, c):
        for k in range(TOP_K):
            _row_copy(x_ref, r, xbuf_ref, dest_ref[TOP_K * r + k], sem).start()
        return c

    lax.fori_loop(0, tm, issue, 0, unroll=8)

    @pl.when(pl.program_id(0) == pl.num_programs(0) - 1)
    def _():
        zero_ref[...] = jnp.zeros(zero_ref.shape, zero_ref.dtype)

        def zissue(r, c):
            _row_copy(zero_ref, 0, xbuf_ref, fill_ref[r], zsem).start()
            return c

        lax.fori_loop(0, n_fill, zissue, 0, unroll=8)

        def zwait(r, c):
            pltpu.make_async_copy(zero_ref, xbuf_ref.at[pl.ds(0, zrows)], zsem).wait()
            return c

        lax.fori_loop(0, n_fill // zrows, zwait, 0)

    for k in range(TOP_K):
        pltpu.make_async_copy(x_ref, xbuf_ref.at[pl.ds(0, tm)], sem).wait()


def _dispatch(x2, dest, fill, *, rows, tm):
    n, d = x2.shape
    n_fill = fill.shape[0]
    zrows = 8
    assert n_fill % zrows == 0
    return pl.pallas_call(
        functools.partial(_dispatch_kernel, n_fill=n_fill),
        grid_spec=pltpu.PrefetchScalarGridSpec(
            num_scalar_prefetch=1,
            grid=(n // tm,),
            in_specs=[pl.BlockSpec((TOP_K * tm,), lambda i, f: (i,), memory_space=pltpu.SMEM),
                      pl.BlockSpec((tm, d), lambda i, f: (i, 0))],
            out_specs=pl.BlockSpec(memory_space=pl.ANY),
            scratch_shapes=[pltpu.VMEM((zrows, d), _F32), pltpu.SemaphoreType.DMA, pltpu.SemaphoreType.DMA],
        ),
        out_shape=jax.ShapeDtypeStruct((rows, d), _F32),
        compiler_params=_cparams("arbitrary"),
        name="moe_dispatch",
    )(fill, dest, x2)


def _combine_kernel(dest_ref, next_ref, x_ref, info_ref, ybuf_ref, g_ref, b_ref, o_ref, y_ref, sems):
    i = pl.program_id(0)
    tm = x_ref.shape[0]
    slot = i % 2
    nslot = 1 - slot

    def gather_tile(idx_ref, to_slot):
        def issue(r, c):
            for k in range(TOP_K):
                _row_copy(ybuf_ref, idx_ref[TOP_K * r + k], y_ref.at[to_slot, k], r, sems.at[to_slot, k]).start()
            return c

        lax.fori_loop(0, tm, issue, 0, unroll=8)

    def wait_tile(in_slot):
        for k in range(TOP_K):
            pltpu.make_async_copy(ybuf_ref.at[pl.ds(0, tm)], y_ref.at[in_slot, k], sems.at[in_slot, k]).wait()

    @pl.when(i == 0)
    def _():
        gather_tile(dest_ref, slot)

    gather_tile(next_ref, nslot)
    wait_tile(slot)
    info = info_ref[...]
    f = y_ref[slot, 0] * info[:, 2 * TOP_K:2 * TOP_K + 1]
    for k in range(1, TOP_K):
        f = f + y_ref[slot, k] * info[:, 2 * TOP_K + k:2 * TOP_K + k + 1]
    o_ref[...] = _layer_norm(ALPHA * x_ref[...] + f, g_ref[...], b_ref[...])

    @pl.when(i == pl.num_programs(0) - 1)
    def _():
        wait_tile(nslot)


def _combine(x2, info, dest, ybuf, g, b, *, tm):
    n, d = x2.shape
    n_tiles = n // tm
    row = pl.BlockSpec((tm, d), lambda i: (i, 0))
    idx = lambda f: pl.BlockSpec((TOP_K * tm,), f, memory_space=pltpu.SMEM)
    return pl.pallas_call(
        _combine_kernel,
        grid=(n_tiles,),
        in_specs=[idx(lambda i: (i,)), idx(lambda i: (jnp.minimum(i + 1, n_tiles - 1),)), row,
                  pl.BlockSpec((tm, ROUTE_LANES), lambda i: (i, 0)), pl.BlockSpec(memory_space=pl.ANY),
                  _full((1, d)), _full((1, d))],
        out_specs=row,
        out_shape=jax.ShapeDtypeStruct((n, d), _F32),
        scratch_shapes=[pltpu.VMEM((2, TOP_K, tm, d), _F32), pltpu.SemaphoreType.DMA((2, TOP_K))],
        compiler_params=_cparams("arbitrary"),
        name="moe_combine",
    )(dest, dest, x2, info, ybuf, g[None], b[None])


def _moe(x2, w_r, b_r, g, b, *, tm, cast=None, experts=None):
    n, d = x2.shape
    n_assign = n * TOP_K
    blk = min(MOE_BLOCK, max(8, -(-n_assign // (8 * N_EXPERTS)) * 8))
    n_blocks = -(-(n_assign + N_EXPERTS * (blk - 1)) // blk)
    if experts is None:
        info, cnt, *experts = _router(x2, w_r, b_r, tm=tm, cast=cast)
    else:
        info, cnt = _router(x2, w_r, b_r, tm=tm)
    dest, blk_e, nblk, fill = _route_plan(info, cnt, blk=blk, n_blocks=n_blocks)
    xbuf = _dispatch(x2, dest, fill, rows=n_blocks * blk, tm=tm)
    ybuf = _moe_experts(xbuf, blk_e, nblk, *experts, blk=blk)
    return _combine(x2, info, dest, ybuf, g, b, tm=tm), tuple(experts)


def kernel(x_prompt, x_sample, cache_b_k, cache_b_v, state_c_conv, ln_g, ln_b, w_a_in, b_a_in, ln_a_g, ln_a_b,
           w_a_s, b_a_s, w_a_out, w_b_qkv, b_b_qkv, b_sink, w_b_out, w_c_in, b_c_in, w_c_dw, b_c_dw, ln_c_g,
           ln_c_b, w_c_out, w_f_in, w_f_out, w_r, b_r, w_e_in, w_e_out):
    bsz, t, d = x_prompt.shape
    dbs, s, _ = x_sample.shape
    n_p, n_s = bsz * t, dbs * s
    xp = x_prompt.reshape(n_p, d)
    xs = x_sample.reshape(n_s, d)
    a_v_s, b_k_p, b_v_p, b_k_s, b_v_s, c_p, c_s = [], [], [], [], [], [], []
    w_f_in_b, w_f_out_b = w_f_in.astype(_BF16), w_f_out.astype(_BF16)
    for i in range(DEPTH):
        kind, j = i % N_MIXERS, i // N_MIXERS
        g0, b0 = ln_g[i, 0], ln_b[i, 0]
        if kind == 0:
            args = (w_a_in[j], b_a_in[j], ln_a_g[j], ln_a_b[j], w_a_s[j], b_a_s[j], w_a_out[j], g0, b0)
            (xp,) = _gmlp(xp, *args, chunk=A_CHUNK, tm=ROW_TILE, emit_v=False)
            xs, v_rows = _gmlp(xs, *args, chunk=s, tm=n_s, emit_v=True)
            a_v_s.append(v_rows.reshape(dbs, s, d))
        elif kind == 1:
            args = (w_b_qkv[j], b_b_qkv[j], b_sink[j], w_b_out[j], g0, b0)
            xp3, kp, vp = _swa_prompt(xp.reshape(bsz, t, d), *args, tm=ROW_TILE)
            xs3, kn, vn = _swa_sample(xs.reshape(dbs, s, d), cache_b_k[j], cache_b_v[j], *args)
            xp, xs = xp3.reshape(n_p, d), xs3.reshape(n_s, d)
            b_k_p.append(kp)
            b_v_p.append(vp)
            b_k_s.append(kn)
            b_v_s.append(vn)
        else:
            args = (w_c_in[j], b_c_in[j], w_c_dw[j], b_c_dw[j], ln_c_g[j], ln_c_b[j], w_c_out[j], g0, b0)
            zero_state = jnp.zeros((bsz, CONV_WIDTH - 1, d), _F32)
            xp3, cp = _conv(xp.reshape(bsz, t, d), zero_state, *args, tm=ROW_TILE)
            xs3, cs = _conv(xs.reshape(dbs, s, d), state_c_conv[j], *args, tm=s)
            xp, xs = xp3.reshape(n_p, d), xs3.reshape(n_s, d)
            c_p.append(cp)
            c_s.append(cs)
        f = i // 2
        g1, b1 = ln_g[i, 1], ln_b[i, 1]
        if i % 2 == 0:
            xp = _ffn(xp, w_f_in_b, w_f_out_b, g1, b1, layer=f, tm=FFN_ROW_TILE)
            xs = _ffn(xs, w_f_in_b, w_f_out_b, g1, b1, layer=f, tm=n_s)
        else:
            xp, experts = _moe(xp, w_r[f], b_r[f], g1, b1, tm=ROW_TILE, cast=(w_e_in, w_e_out, f))
            xs, _ = _moe(xs, w_r[f], b_r[f], g1, b1, tm=n_s, experts=experts)
    return (xp.reshape(bsz, t, d), xs.reshape(dbs, s, d), jnp.stack(a_v_s), jnp.stack(b_k_p),
            jnp.stack(b_v_p), jnp.stack(b_k_s), jnp.stack(b_v_s), jnp.stack(c_p), jnp.stack(c_s))
```

```python
import functools

import jax
import jax.numpy as jnp
import numpy as np
from jax import lax
from jax.experimental import pallas as pl
from jax.experimental.pallas import tpu as pltpu

D_MODEL = 1024
DEPTH = 4
PAST_LEN = 1024
CHUNK = 64
N_MIXERS = 3
ALPHA = (2 * DEPTH) ** 0.25
LN_EPS = 1e-5
A_CHUNK = 128
A_GROUPS = 4
HEAD_DIM = 64
B_Q_HEADS = D_MODEL // HEAD_DIM
B_KV_HEADS = 4
B_GROUP = B_Q_HEADS // B_KV_HEADS
WINDOW = 128
BAND_CHUNKS = WINDOW // CHUNK
ROT_DIM = HEAD_DIM // 4
ROPE_THETA = 500000.0
CONV_WIDTH = 31
N_EXPERTS = 8
TOP_K = 2
MOE_BLOCK = 512

VMEM_LIMIT_BYTES = 56 * 1024 * 1024
ROW_TILE = 512
FFN_ROW_TILE = 1024
FF_SUB = 256
CONV_HALO = 32
GROUP_LANES = B_GROUP * HEAD_DIM
ROPE_LANES = 128

_BF16 = jnp.bfloat16
_F32 = jnp.float32


def _cparams(*sem):
    return pltpu.CompilerParams(dimension_semantics=sem, vmem_limit_bytes=VMEM_LIMIT_BYTES)


def _dot(a, b):
    return jnp.dot(a, b, preferred_element_type=_F32)


def _layer_norm(x, g, b):
    mu = jnp.mean(x, axis=-1, keepdims=True)
    xc = x - mu
    var = jnp.mean(xc * xc, axis=-1, keepdims=True)
    return xc * lax.rsqrt(var + LN_EPS) * g + b


def _sigmoid(x):
    return 1.0 / (1.0 + jnp.exp(-x))


def _full(shape):
    nd = len(shape)
    return pl.BlockSpec(shape, lambda *_: (0,) * nd)


def _gmlp_kernel(x_ref, w_in_ref, b_in_ref, g_a_ref, b_a_ref, w_s_ref, bz_ref, w_out_ref,
                 g_ref, b_ref, *out_refs, chunk, emit_v):
    o_ref = out_refs[0]
    x = x_ref[...]
    tm = x.shape[0]
    h = _dot(x.astype(_BF16), w_in_ref[...]) + b_in_ref[...]
    h = jax.nn.gelu(h)
    u = h[:, :D_MODEL]
    v = _layer_norm(h[:, D_MODEL:], g_a_ref[...], b_a_ref[...])
    if emit_v:
        out_refs[1][...] = v
    vb = v.astype(_BF16)
    ri = lax.broadcasted_iota(jnp.int32, (chunk, chunk), 0) // CHUNK
    ci = lax.broadcasted_iota(jnp.int32, (chunk, chunk), 1) // CHUNK
    gw = D_MODEL // A_GROUPS
    ws = [jnp.where(ci <= ri, w_s_ref[g], 0.0).astype(_BF16) for g in range(A_GROUPS)]
    rows = []
    for n in range(tm // chunk):
        r0 = n * chunk
        cols = [_dot(ws[g], vb[r0:r0 + chunk, g * gw:(g + 1) * gw]) for g in range(A_GROUPS)]
        rows.append(jnp.concatenate(cols, axis=1) + bz_ref[...])
    z = jnp.concatenate(rows, axis=0) if len(rows) > 1 else rows[0]
    y = _dot((u * z).astype(_BF16), w_out_ref[...])
    o_ref[...] = _layer_norm(ALPHA * x + y, g_ref[...], b_ref[...])


def _gmlp(x2, w_in, b_in, g_a, b_a, w_s, b_s, w_out, g, b, *, chunk, tm, emit_v):
    n = x2.shape[0]
    d = D_MODEL
    gw = d // A_GROUPS
    bz = jnp.repeat(b_s[:, :chunk].T, gw, axis=1)
    out_shape = [jax.ShapeDtypeStruct((n, d), _F32)]
    out_specs = [pl.BlockSpec((tm, d), lambda i: (i, 0))]
    if emit_v:
        out_shape.append(jax.ShapeDtypeStruct((n, d), _F32))
        out_specs.append(pl.BlockSpec((tm, d), lambda i: (i, 0)))
    res = pl.pallas_call(
        functools.partial(_gmlp_kernel, chunk=chunk, emit_v=emit_v),
        grid=(n // tm,),
        in_specs=[
            pl.BlockSpec((tm, d), lambda i: (i, 0)),
            _full((d, 2 * d)), _full((1, 2 * d)), _full((1, d)), _full((1, d)),
            _full((A_GROUPS, chunk, chunk)), _full((chunk, d)), _full((d, d)),
            _full((1, d)), _full((1, d)),
        ],
        out_specs=out_specs,
        out_shape=out_shape,
        compiler_params=_cparams("arbitrary"),
        name="gmlp_mixer",
    )(x2, w_in.astype(_BF16), b_in[None], g_a[None], b_a[None], w_s[:, :chunk, :chunk], bz,
      w_out.astype(_BF16), g[None], b[None])
    return res


def _rope_tables(pos):
    half = ROT_DIM // 2
    inv = ROPE_THETA ** (-np.arange(0, ROT_DIM, 2, dtype=np.float32) / ROT_DIM)
    lane = np.arange(ROPE_LANES) % HEAD_DIM
    inv_lane = jnp.asarray(inv[lane % half], _F32)[None, :]
    lo = jnp.asarray(lane < half)[None, :]
    hi = jnp.asarray((lane >= half) & (lane < ROT_DIM))[None, :]
    ang = pos.astype(_F32)[:, None] * inv_lane
    cos, sin = jnp.cos(ang), jnp.sin(ang)
    c = jnp.where(lo | hi, cos, 1.0)
    s_up = jnp.where(hi, sin, 0.0)
    s_dn = jnp.where(lo, -sin, 0.0)
    return c, s_up, s_dn


def _rotary(x, c, s_up, s_dn):
    half = ROT_DIM // 2
    slabs = []
    for i in range(x.shape[1] // ROPE_LANES):
        xs = x[:, i * ROPE_LANES:(i + 1) * ROPE_LANES]
        slabs.append(xs * c + pltpu.roll(xs, half, axis=1) * s_up
                     + pltpu.roll(xs, ROPE_LANES - half, axis=1) * s_dn)
    return jnp.concatenate(slabs, axis=1)


def _expand_kv_weights(w_qkv, b_qkv):
    nq = B_Q_HEADS * HEAD_DIM
    nk = B_KV_HEADS * HEAD_DIM
    scale = HEAD_DIM ** -0.5
    assert float(np.log2(scale)).is_integer()

    def rep(a):
        lead = a.shape[:-1]
        a = a.reshape(lead + (B_KV_HEADS, 1, HEAD_DIM))
        a = jnp.broadcast_to(a, lead + (B_KV_HEADS, B_GROUP, HEAD_DIM))
        return a.reshape(lead + (nq,))

    w = jnp.concatenate([w_qkv[:, :nq] * scale, rep(w_qkv[:, nq:nq + nk]), rep(w_qkv[:, nq + nk:])], axis=1)
    bb = jnp.concatenate([b_qkv[:nq] * scale, rep(b_qkv[nq:nq + nk]), rep(b_qkv[nq + nk:])])
    return w, bb


def _attend(q, kwin, vwin, valid, sink_col):
    nq = q.shape[0]
    lane_head = lax.broadcasted_iota(jnp.int32, (1, GROUP_LANES), 1) // HEAD_DIM
    masks = [(lane_head == g).astype(_F32) for g in range(B_GROUP)]
    qs = jnp.concatenate([q * masks[g] for g in range(B_GROUP)], axis=0).astype(_BF16)
    s = lax.dot_general(qs, kwin, (((1,), (1,)), ((), ())), preferred_element_type=_F32)
    if valid is not None:
        s = jnp.where(valid, s, -1e30)
    m = jnp.maximum(jnp.max(s, axis=-1, keepdims=True), sink_col)
    p = jnp.exp(s - m)
    p = p / (jnp.sum(p, axis=-1, keepdims=True) + jnp.exp(sink_col - m))
    o = _dot(p.astype(_BF16), vwin)
    out = o[0:nq] * masks[0]
    for g in range(1, B_GROUP):
        out = out + o[g * nq:(g + 1) * nq] * masks[g]
    return out


def _sink_cols(sink_ref, hk, nq):
    return jnp.concatenate(
        [jnp.full((nq, 1), sink_ref[hk * B_GROUP + g], _F32) for g in range(B_GROUP)], axis=0)


SCORE_LANES = 256
SOFTMAX_ROWS = 512


def _swa_prompt_kernel(sink_ref, x_ref, w_ref, bq_ref, c_ref, su_ref, sd_ref, w_out_ref, g_ref, b_ref,
                       o_ref, kc_ref, vc_ref, k_ext, v_ext, s_ref, p_ref):
    t = pl.program_id(1)
    tm = x_ref.shape[0]
    d = D_MODEL
    halo = BAND_CHUNKS * CHUNK
    nkeys = (BAND_CHUNKS + 1) * CHUNK
    n_chunks = tm // CHUNK
    grp = B_GROUP * CHUNK

    @pl.when(t == 0)
    def _():
        k_ext[0:halo, :] = jnp.zeros((halo, d), _BF16)
        v_ext[0:halo, :] = jnp.zeros((halo, d), _BF16)
        k_ext[halo + tm:, :] = jnp.zeros((k_ext.shape[0] - halo - tm, d), _BF16)

    x = x_ref[...]
    qkv = _dot(x.astype(_BF16), w_ref[...]) + bq_ref[...]
    qk = _rotary(qkv[:, :2 * d], c_ref[...], su_ref[...], sd_ref[...])
    q = qk[:, :d]
    k = qk[:, d:]
    v = qkv[:, 2 * d:]
    k_ext[halo:halo + tm, :] = k.astype(_BF16)
    v_ext[halo:halo + tm, :] = v.astype(_BF16)
    kc_ref[...] = k[tm - WINDOW:, :]
    vc_ref[...] = v[tm - WINDOW:, :]

    lane_head = lax.broadcasted_iota(jnp.int32, (1, GROUP_LANES), 1) // HEAD_DIM
    masks = [(lane_head == g).astype(_F32) for g in range(B_GROUP)]
    key_lane = lax.broadcasted_iota(jnp.int32, (grp, SCORE_LANES), 1)
    tails = [jnp.where(key_lane == nkeys, _sink_cols(sink_ref, hk, CHUNK), -1e30) for hk in range(B_KV_HEADS)]

    for n in range(n_chunks):
        r0 = n * CHUNK
        if n < BAND_CHUNKS:
            first = jnp.maximum((BAND_CHUNKS - n - t * n_chunks) * CHUNK, 0)
            keep = (lax.bitcast_convert_type(key_lane - first, jnp.uint32)
                    < lax.bitcast_convert_type(nkeys - first, jnp.uint32))
        else:
            keep = key_lane < nkeys
        for hk in range(B_KV_HEADS):
            l0 = hk * GROUP_LANES
            qn = q[r0:r0 + CHUNK, l0:l0 + GROUP_LANES]
            qs = jnp.concatenate([qn * masks[g] for g in range(B_GROUP)], axis=0).astype(_BF16)
            s = lax.dot_general(qs, k_ext[r0:r0 + SCORE_LANES, l0:l0 + GROUP_LANES],
                                (((1,), (1,)), ((), ())), preferred_element_type=_F32)
            i0 = (n * B_KV_HEADS + hk) * grp
            s_ref[i0:i0 + grp, :] = jnp.where(keep, s, tails[hk])

    for i in range(s_ref.shape[0] // SOFTMAX_ROWS):
        s = s_ref[i * SOFTMAX_ROWS:(i + 1) * SOFTMAX_ROWS, :]
        p = jnp.exp(s - jnp.max(s, axis=-1, keepdims=True))
        p_ref[i * SOFTMAX_ROWS:(i + 1) * SOFTMAX_ROWS, :] = (p / jnp.sum(p, axis=-1, keepdims=True)).astype(_BF16)

    rows = []
    for n in range(n_chunks):
        r0 = n * CHUNK
        cols = []
        for hk in range(B_KV_HEADS):
            l0 = hk * GROUP_LANES
            i0 = (n * B_KV_HEADS + hk) * grp
            o = _dot(p_ref[i0:i0 + grp, 0:nkeys], v_ext[r0:r0 + nkeys, l0:l0 + GROUP_LANES])
            out = o[0:CHUNK] * masks[0]
            for g in range(1, B_GROUP):
                out = out + o[g * CHUNK:(g + 1) * CHUNK] * masks[g]
            cols.append(out)
        rows.append(jnp.concatenate(cols, axis=1))
    att = jnp.concatenate(rows, axis=0)
    y = _dot(att.astype(_BF16), w_out_ref[...])
    o_ref[...] = _layer_norm(ALPHA * x + y, g_ref[...], b_ref[...])
    k_ext[0:halo, :] = k_ext[tm:tm + halo, :]
    v_ext[0:halo, :] = v_ext[tm:tm + halo, :]


def _unexpand(a):
    lead = a.shape[:-1]
    return a.reshape(lead + (B_KV_HEADS, B_GROUP, HEAD_DIM))[..., 0, :]


def _swa_prompt(x3, w_qkv, b_qkv, sink, w_out, g, b, *, tm):
    bsz, t, d = x3.shape
    w_exp, b_exp = _expand_kv_weights(w_qkv, b_qkv)
    c, su, sd = _rope_tables(jnp.arange(t))
    tab = pl.BlockSpec((tm, 2 * HEAD_DIM), lambda bi, ti, *_: (ti, 0))
    xspec = pl.BlockSpec((None, tm, d), lambda bi, ti, *_: (bi, ti, 0))
    cspec = pl.BlockSpec((None, WINDOW, d), lambda bi, ti, *_: (bi, 0, 0))
    halo = BAND_CHUNKS * CHUNK
    score_rows = (tm // CHUNK) * B_KV_HEADS * B_GROUP * CHUNK
    assert score_rows % SOFTMAX_ROWS == 0
    one = pl.Buffered(1)
    y, kc, vc = pl.pallas_call(
        _swa_prompt_kernel,
        grid_spec=pltpu.PrefetchScalarGridSpec(
            num_scalar_prefetch=1,
            grid=(bsz, t // tm),
            in_specs=[xspec, pl.BlockSpec((d, 3 * d), lambda *_: (0, 0), pipeline_mode=one),
                      _full((1, 3 * d)), tab, tab, tab,
                      pl.BlockSpec((d, d), lambda *_: (0, 0), pipeline_mode=one),
                      _full((1, d)), _full((1, d))],
            out_specs=[xspec, cspec, cspec],
            scratch_shapes=[
                pltpu.VMEM((halo + tm + SCORE_LANES - (halo + CHUNK), d), _BF16),
                pltpu.VMEM((halo + tm, d), _BF16),
                pltpu.VMEM((score_rows, SCORE_LANES), _F32),
                pltpu.VMEM((score_rows, SCORE_LANES), _BF16)],
        ),
        out_shape=[jax.ShapeDtypeStruct((bsz, t, d), _F32),
                   jax.ShapeDtypeStruct((bsz, WINDOW, d), _F32),
                   jax.ShapeDtypeStruct((bsz, WINDOW, d), _F32)],
        compiler_params=_cparams("arbitrary", "arbitrary"),
        name="swa_prompt_mixer",
    )(sink, x3, w_exp.astype(_BF16), b_exp[None], c, su, sd, w_out.astype(_BF16), g[None], b[None])
    return y, _unexpand(kc), _unexpand(vc)


def _swa_sample_kernel(sink_ref, x_ref, ck_ref, cv_ref, valid_ref, w_ref, bq_ref, c_ref, su_ref, sd_ref,
                       w_out_ref, g_ref, b_ref, o_ref, kn_ref, vn_ref, *, bsz, s):
    d = D_MODEL
    x = x_ref[...]
    qkv = _dot(x.astype(_BF16), w_ref[...]) + bq_ref[...]
    qk = _rotary(qkv[:, :2 * d], c_ref[...], su_ref[...], sd_ref[...])
    q = qk[:, :d]
    k = qk[:, d:]
    v = qkv[:, 2 * d:]
    kn_ref[...] = k
    vn_ref[...] = v
    kb = k.astype(_BF16)
    vb = v.astype(_BF16)
    valid = jnp.concatenate([valid_ref[...] > 0] * B_GROUP, axis=0)
    rows = []
    for bi in range(bsz):
        r0 = bi * s
        kk = jnp.concatenate([ck_ref[bi], kb[r0:r0 + s]], axis=0)
        vv = jnp.concatenate([cv_ref[bi], vb[r0:r0 + s]], axis=0)
        cols = []
        for hk in range(B_KV_HEADS):
            l0 = hk * GROUP_LANES
            cols.append(_attend(q[r0:r0 + s, l0:l0 + GROUP_LANES], kk[:, l0:l0 + GROUP_LANES],
                                vv[:, l0:l0 + GROUP_LANES], valid, _sink_cols(sink_ref, hk, s)))
        rows.append(jnp.concatenate(cols, axis=1))
    att = jnp.concatenate(rows, axis=0)
    y = _dot(att.astype(_BF16), w_out_ref[...])
    o_ref[...] = _layer_norm(ALPHA * x + y, g_ref[...], b_ref[...])


def _swa_sample(x3, ck, cv, w_qkv, b_qkv, sink, w_out, g, b):
    bsz, s, d = x3.shape
    nc = ck.shape[1]
    w_exp, b_exp = _expand_kv_weights(w_qkv, b_qkv)
    qpos = PAST_LEN + np.arange(s)
    kpos = np.concatenate([PAST_LEN - nc + np.arange(nc), qpos])
    qc, kc = qpos // CHUNK, kpos // CHUNK
    valid = ((kc[None, :] <= qc[:, None]) & (kc[None, :] >= qc[:, None] - BAND_CHUNKS)).astype(np.int32)
    c, su, sd = _rope_tables(jnp.tile(jnp.asarray(qpos), bsz))

    def expand_cache(a):
        a = jnp.broadcast_to(a[:, :, :, None, :], (bsz, nc, B_KV_HEADS, B_GROUP, HEAD_DIM))
        return a.reshape(bsz, nc, d).astype(_BF16)

    n = bsz * s
    vm = pl.BlockSpec(memory_space=pltpu.VMEM)
    y, kn, vn = pl.pallas_call(
        functools.partial(_swa_sample_kernel, bsz=bsz, s=s),
        in_specs=[pl.BlockSpec(memory_space=pltpu.SMEM)] + [vm] * 12,
        out_specs=[vm, vm, vm],
        out_shape=[jax.ShapeDtypeStruct((n, d), _F32)] * 3,
        compiler_params=pltpu.CompilerParams(vmem_limit_bytes=VMEM_LIMIT_BYTES),
        name="swa_sample_mixer",
    )(sink, x3.reshape(n, d), expand_cache(ck), expand_cache(cv), jnp.asarray(valid),
      w_exp.astype(_BF16), b_exp[None], c, su, sd, w_out.astype(_BF16), g[None], b[None])
    return (y.reshape(bsz, s, d), _unexpand(kn.reshape(bsz, s, d)), _unexpand(vn.reshape(bsz, s, d)))


def _conv_kernel(x_ref, st_ref, w_in_ref, b_in_ref, w_dw_ref, b_dw_ref, g_c_ref, b_c_ref, w_out_ref,
                 g_ref, b_ref, o_ref, st_out_ref, h_ext):
    t = pl.program_id(1)
    tm = x_ref.shape[0]
    d = D_MODEL

    @pl.when(t == 0)
    def _():
        h_ext[0:CONV_HALO, :] = st_ref[...]

    x = x_ref[...]
    ag = _dot(x.astype(_BF16), w_in_ref[...]) + b_in_ref[...]
    h = ag[:, :d] * _sigmoid(ag[:, d:])
    h_ext[CONV_HALO:CONV_HALO + tm, :] = h
    base = CONV_HALO - (CONV_WIDTH - 1)
    acc = None
    for sub in range(8):
        offs = [o for o in range(base, base + CONV_WIDTH) if o % 8 == sub]
        if not offs:
            continue
        rows = tm + (8 if sub else 0)
        part = None
        for o in offs:
            a0 = o - sub
            term = h_ext[a0:a0 + rows, :] * w_dw_ref[o - base:o - base + 1, :]
            part = term if part is None else part + term
        part = part[sub:sub + tm, :]
        acc = part if acc is None else acc + part
    y = acc + b_dw_ref[...]
    y = _layer_norm(y, g_c_ref[...], b_c_ref[...])
    y = y * _sigmoid(y)
    y = _dot(y.astype(_BF16), w_out_ref[...])
    o_ref[...] = _layer_norm(ALPHA * x + y, g_ref[...], b_ref[...])
    new_hist = h_ext[tm:tm + CONV_HALO, :]
    st_out_ref[...] = new_hist
    h_ext[0:CONV_HALO, :] = new_hist


def _conv(x3, state, w_in, b_in, w_dw, b_dw, g_c, b_c, w_out, g, b, *, tm):
    bsz, t, d = x3.shape
    st = jnp.pad(state, ((0, 0), (CONV_HALO - (CONV_WIDTH - 1), 0), (0, 0)))
    xspec = pl.BlockSpec((None, tm, d), lambda bi, ti: (bi, ti, 0))
    sspec = pl.BlockSpec((None, CONV_HALO, d), lambda bi, ti: (bi, 0, 0))
    y, st_new = pl.pallas_call(
        _conv_kernel,
        grid=(bsz, t // tm),
        in_specs=[xspec, sspec, _full((d, 2 * d)), _full((1, 2 * d)), _full((CONV_WIDTH, d)), _full((1, d)),
                  _full((1, d)), _full((1, d)), _full((d, d)), _full((1, d)), _full((1, d))],
        out_specs=[xspec, sspec],
        out_shape=[jax.ShapeDtypeStruct((bsz, t, d), _F32),
                   jax.ShapeDtypeStruct((bsz, CONV_HALO, d), _F32)],
        scratch_shapes=[pltpu.VMEM((tm + CONV_HALO, d), _F32)],
        compiler_params=_cparams("arbitrary", "arbitrary"),
        name="conv_mixer",
    )(x3, st, w_in.astype(_BF16), b_in[None], w_dw, b_dw[None], g_c[None], b_c[None],
      w_out.astype(_BF16), g[None], b[None])
    return y, st_new[:, CONV_HALO - (CONV_WIDTH - 1):]


def _swiglu_partial(xb, wg_ref, wu_ref, wo_ref, width):
    acc = None
    for c in range(width // FF_SUB):
        sl = slice(c * FF_SUB, (c + 1) * FF_SUB)
        gt = _dot(xb, wg_ref[:, sl])
        up = _dot(xb, wu_ref[:, sl])
        hh = (gt * _sigmoid(gt) * up).astype(_BF16)
        part = _dot(hh, wo_ref[sl, :])
        acc = part if acc is None else acc + part
    return acc


def _ffn_kernel(x_ref, wg_ref, wu_ref, wo_ref, g_ref, b_ref, o_ref, *, d_ff):
    x = x_ref[...]
    f = _swiglu_partial(x.astype(_BF16), wg_ref, wu_ref, wo_ref, d_ff)
    o_ref[...] = _layer_norm(ALPHA * x + f, g_ref[...], b_ref[...])


def _ffn(x2, w_in, w_out, g, b, *, layer, tm):
    n, d = x2.shape
    d_ff = w_out.shape[1]
    one = pl.Buffered(1)
    return pl.pallas_call(
        functools.partial(_ffn_kernel, d_ff=d_ff),
        grid=(n // tm,),
        in_specs=[
            pl.BlockSpec((tm, d), lambda i: (i, 0)),
            pl.BlockSpec((None, d, d_ff), lambda i: (layer, 0, 0), pipeline_mode=one),
            pl.BlockSpec((None, d, d_ff), lambda i: (layer, 0, 1), pipeline_mode=one),
            pl.BlockSpec((None, d_ff, d), lambda i: (layer, 0, 0), pipeline_mode=one),
            _full((1, d)), _full((1, d)),
        ],
        out_specs=pl.BlockSpec((tm, d), lambda i: (i, 0)),
        out_shape=jax.ShapeDtypeStruct((n, d), _F32),
        compiler_params=_cparams("arbitrary"),
        name="dense_swiglu",
    )(x2, w_in, w_in, w_out, g[None], b[None])


def _moe_kernel(blk_e_ref, nblk_ref, x_ref, wg_ref, wu_ref, wo_ref, o_ref, acc_ref, *, width):
    bi = pl.program_id(0)
    j = pl.program_id(1)
    last = pl.num_programs(1) - 1
    live = bi < nblk_ref[0]

    @pl.when(live)
    def _():
        part = _swiglu_partial(x_ref[...].astype(_BF16), wg_ref, wu_ref, wo_ref, width)

        @pl.when(j == 0)
        def _():
            acc_ref[...] = part

        @pl.when((j > 0) & (j < last))
        def _():
            acc_ref[...] += part

        @pl.when(j == last)
        def _():
            o_ref[...] = acc_ref[...] + part

    @pl.when(jnp.logical_not(live) & (j == last))
    def _():
        o_ref[...] = jnp.zeros(o_ref.shape, o_ref.dtype)


def _moe_experts(xbuf, blk_e, nblk, w_e_in, w_e_out, *, blk):
    rows, d = xbuf.shape
    n_blocks = rows // blk
    d_ff = w_e_out.shape[1]
    splits = 2
    width = d_ff // splits

    def wj(i, j, nb):
        return jnp.where(i < nb[0], j, splits - 1)

    return pl.pallas_call(
        functools.partial(_moe_kernel, width=width),
        grid_spec=pltpu.PrefetchScalarGridSpec(
            num_scalar_prefetch=2,
            grid=(n_blocks, splits),
            in_specs=[
                pl.BlockSpec((blk, d), lambda i, j, e, nb: (i, 0)),
                pl.BlockSpec((None, d, width), lambda i, j, e, nb: (e[i], 0, wj(i, j, nb))),
                pl.BlockSpec((None, d, width), lambda i, j, e, nb: (e[i], 0, splits + wj(i, j, nb))),
                pl.BlockSpec((None, width, d), lambda i, j, e, nb: (e[i], wj(i, j, nb), 0)),
            ],
            out_specs=pl.BlockSpec((blk, d), lambda i, j, e, nb: (i, 0)),
            scratch_shapes=[pltpu.VMEM((blk, d), _F32)],
        ),
        out_shape=jax.ShapeDtypeStruct((rows, d), _F32),
        compiler_params=_cparams("arbitrary", "arbitrary"),
        name="moe_experts",
    )(blk_e, nblk, xbuf, w_e_in, w_e_in, w_e_out)


ROUTE_LANES = 128


def _router_kernel(x_ref, wh_ref, wl_ref, br_ref, *rest, cast_weights):
    if cast_weights:
        w_ref, info_ref, cnt_ref, w_b_ref, base_ref = rest
        w_b_ref[...] = w_ref[...].astype(_BF16)
    else:
        info_ref, cnt_ref, base_ref = rest

    @pl.when(pl.program_id(0) == 0)
    def _():
        base_ref[...] = jnp.zeros(base_ref.shape, _F32)

    x = x_ref[...]
    tm = x.shape[0]
    xh = x.astype(_BF16)
    xl = (x - xh.astype(_F32)).astype(_BF16)
    logits = _dot(xh, wh_ref[...]) + (_dot(xh, wl_ref[...]) + _dot(xl, wh_ref[...])) + br_ref[...]
    lane = lax.broadcasted_iota(jnp.int32, logits.shape, 1)
    neg = -jnp.inf
    logits = jnp.where(lane < N_EXPERTS, logits, neg)
    m1 = jnp.max(logits, axis=-1, keepdims=True)
    i1 = jnp.min(jnp.where(logits == m1, lane, ROUTE_LANES), axis=-1, keepdims=True)
    rest = jnp.where(lane == i1, neg, logits)
    m2 = jnp.max(rest, axis=-1, keepdims=True)
    i2 = jnp.min(jnp.where(rest == m2, lane, ROUTE_LANES), axis=-1, keepdims=True)
    ex = jnp.exp(m2 - m1)
    g1 = 1.0 / (1.0 + ex)
    g2 = ex / (1.0 + ex)
    oh1 = (lane == i1).astype(_F32)
    oh2 = (lane == i2).astype(_F32)
    oh = oh1 + oh2
    tri = (lax.broadcasted_iota(jnp.int32, (tm, tm), 0) > lax.broadcasted_iota(jnp.int32, (tm, tm), 1))
    before = _dot(tri.astype(_BF16), oh.astype(_BF16)) + base_ref[...]
    r1 = jnp.sum(before * oh1, axis=-1, keepdims=True)
    r2 = jnp.sum(before * oh2, axis=-1, keepdims=True)
    base_ref[...] += jnp.sum(oh, axis=0, keepdims=True)
    cnt_ref[...] = base_ref[...]
    cols = (i1.astype(_F32), i2.astype(_F32), r1, r2, g1, g2)
    info = jnp.zeros(logits.shape, _F32)
    for c, val in enumerate(cols):
        info = jnp.where(lane == c, val, info)
    info_ref[...] = info


def _cast_specs(w, layer, n_tiles):
    n_e = w.shape[1]
    per_e = n_tiles // n_e
    rows, cols = w.shape[2] // per_e, w.shape[3]
    assert n_tiles == per_e * n_e and w.shape[2] == rows * per_e and rows % 16 == 0
    w_spec = pl.BlockSpec((None, None, rows, cols), lambda i, *_: (layer, i // per_e, i % per_e, 0))
    wb_spec = pl.BlockSpec((None, rows, cols), lambda i, *_: (i // per_e, i % per_e, 0))
    return w_spec, wb_spec, jax.ShapeDtypeStruct(w.shape[1:], _BF16)


def _router(x2, w_r, b_r, *, tm, cast=None):
    n, d = x2.shape
    n_tiles = n // tm
    wp = jnp.pad(w_r, ((0, 0), (0, ROUTE_LANES - N_EXPERTS)))
    wh = wp.astype(_BF16)
    wl = (wp - wh.astype(_F32)).astype(_BF16)
    bp = jnp.pad(b_r, (0, ROUTE_LANES - N_EXPERTS))[None]
    in_specs = [pl.BlockSpec((tm, d), lambda i: (i, 0)), _full((d, ROUTE_LANES)), _full((d, ROUTE_LANES)),
                _full((1, ROUTE_LANES))]
    out_specs = [pl.BlockSpec((tm, ROUTE_LANES), lambda i: (i, 0)), _full((1, ROUTE_LANES))]
    out_shape = [jax.ShapeDtypeStruct((n, ROUTE_LANES), _F32), jax.ShapeDtypeStruct((1, ROUTE_LANES), _F32)]
    args = [x2, wh, wl, bp]
    if cast is not None:
        w_spec, wb_spec, wb_shape = _cast_specs(*cast, n_tiles)
        in_specs.append(w_spec)
        out_specs.append(wb_spec)
        out_shape.append(wb_shape)
        args.append(cast[0])
    return pl.pallas_call(
        functools.partial(_router_kernel, cast_weights=cast is not None),
        grid=(n_tiles,),
        in_specs=in_specs,
        out_specs=out_specs,
        out_shape=out_shape,
        scratch_shapes=[pltpu.VMEM((1, ROUTE_LANES), _F32)],
        compiler_params=_cparams("arbitrary"),
        name="moe_router",
    )(*args)


def _route_plan(info, cnt, *, blk, n_blocks):
    e = info[:, 0:TOP_K].astype(jnp.int32)
    rank = info[:, TOP_K:2 * TOP_K].astype(jnp.int32)
    counts = cnt[0, :N_EXPERTS].astype(jnp.int32)
    padded = (counts + blk - 1) // blk * blk
    pad_end = jnp.cumsum(padded)
    pad_start = pad_end - padded
    dest = (pad_start[e] + rank).reshape(-1)
    nblk = pad_end[-1] // blk
    def owner(bounds, q):
        return jnp.minimum(jnp.sum(bounds[None, :] <= q[:, None], axis=1), N_EXPERTS - 1)

    blk_e = owner(pad_end, jnp.minimum(jnp.arange(n_blocks), nblk - 1) * blk).astype(jnp.int32)
    n_fill = n_blocks * blk - dest.shape[0]
    tail = padded - counts
    gaps = jnp.cumsum(tail)
    slot = jnp.arange(n_fill)
    ge = owner(gaps, slot)
    in_gap = pad_start[ge] + counts[ge] + slot - (gaps[ge] - tail[ge])
    fill = jnp.where(slot < gaps[-1], in_gap, pad_end[-1] + slot - gaps[-1]).astype(jnp.int32)
    return dest.astype(jnp.int32), blk_e, nblk.astype(jnp.int32)[None], fill


def _row_copy(src_ref, src_row, dst_ref, dst_row, sem):
    return pltpu.make_async_copy(src_ref.at[pl.ds(src_row, 1)], dst_ref.at[pl.ds(dst_row, 1)], sem)


def _dispatch_kernel(fill_ref, dest_ref, x_ref, *rest, n_fill, cast_weights):
    if cast_weights:
        w_ref, xbuf_ref, w_b_ref, zero_ref, sem, zsem = rest
        w_b_ref[...] = w_ref[...].astype(_BF16)
    else:
        xbuf_ref, zero_ref, sem, zsem = rest
    tm = x_ref.shape[0]
    zrows = zero_ref.shape[0]

    def issue(r, c):
        for k in range(TOP_K):
            _row_copy(x_ref, r, xbuf_ref, dest_ref[TOP_K * r + k], sem).start()
        return c

    lax.fori_loop(0, tm, issue, 0, unroll=8)

    @pl.when(pl.program_id(0) == pl.num_programs(0) - 1)
    def _():
        zero_ref[...] = jnp.zeros(zero_ref.shape, zero_ref.dtype)

        def zissue(r, c):
            _row_copy(zero_ref, 0, xbuf_ref, fill_ref[r], zsem).start()
            return c

        lax.fori_loop(0, n_fill, zissue, 0, unroll=8)

        def zwait(r, c):
            pltpu.make_async_copy(zero_ref, xbuf_ref.at[pl.ds(0, zrows)], zsem).wait()
            return c

        lax.fori_loop(0, n_fill // zrows, zwait, 0)

    for k in range(TOP_K):
        pltpu.make_async_copy(x_ref, xbuf_ref.at[pl.ds(0, tm)], sem).wait()


def _dispatch(x2, dest, fill, *, rows, tm, cast=None):
    n, d = x2.shape
    n_tiles = n // tm
    n_fill = fill.shape[0]
    zrows = 8
    assert n_fill % zrows == 0
    in_specs = [pl.BlockSpec((TOP_K * tm,), lambda i, f: (i,), memory_space=pltpu.SMEM),
                pl.BlockSpec((tm, d), lambda i, f: (i, 0))]
    out_specs = [pl.BlockSpec(memory_space=pl.ANY)]
    out_shape = [jax.ShapeDtypeStruct((rows, d), _F32)]
    args = [fill, dest, x2]
    if cast is not None:
        w_spec, wb_spec, wb_shape = _cast_specs(*cast, n_tiles)
        in_specs.append(w_spec)
        out_specs.append(wb_spec)
        out_shape.append(wb_shape)
        args.append(cast[0])
    return pl.pallas_call(
        functools.partial(_dispatch_kernel, n_fill=n_fill, cast_weights=cast is not None),
        grid_spec=pltpu.PrefetchScalarGridSpec(
            num_scalar_prefetch=1,
            grid=(n_tiles,),
            in_specs=in_specs,
            out_specs=out_specs,
            scratch_shapes=[pltpu.VMEM((zrows, d), _F32), pltpu.SemaphoreType.DMA, pltpu.SemaphoreType.DMA],
        ),
        out_shape=out_shape,
        compiler_params=_cparams("arbitrary"),
        name="moe_dispatch",
    )(*args)


def _combine_kernel(dest_ref, next_ref, x_ref, info_ref, ybuf_ref, g_ref, b_ref, o_ref, y_ref, sems):
    i = pl.program_id(0)
    tm = x_ref.shape[0]
    slot = i % 2
    nslot = 1 - slot

    def gather_tile(idx_ref, to_slot):
        def issue(r, c):
            for k in range(TOP_K):
                _row_copy(ybuf_ref, idx_ref[TOP_K * r + k], y_ref.at[to_slot, k], r, sems.at[to_slot, k]).start()
            return c

        lax.fori_loop(0, tm, issue, 0, unroll=8)

    def wait_tile(in_slot):
        for k in range(TOP_K):
            pltpu.make_async_copy(ybuf_ref.at[pl.ds(0, tm)], y_ref.at[in_slot, k], sems.at[in_slot, k]).wait()

    @pl.when(i == 0)
    def _():
        gather_tile(dest_ref, slot)

    gather_tile(next_ref, nslot)
    wait_tile(slot)
    info = info_ref[...]
    f = y_ref[slot, 0] * info[:, 2 * TOP_K:2 * TOP_K + 1]
    for k in range(1, TOP_K):
        f = f + y_ref[slot, k] * info[:, 2 * TOP_K + k:2 * TOP_K + k + 1]
    o_ref[...] = _layer_norm(ALPHA * x_ref[...] + f, g_ref[...], b_ref[...])

    @pl.when(i == pl.num_programs(0) - 1)
    def _():
        wait_tile(nslot)


def _combine(x2, info, dest, ybuf, g, b, *, tm):
    n, d = x2.shape
    n_tiles = n // tm
    row = pl.BlockSpec((tm, d), lambda i: (i, 0))
    idx = lambda f: pl.BlockSpec((TOP_K * tm,), f, memory_space=pltpu.SMEM)
    return pl.pallas_call(
        _combine_kernel,
        grid=(n_tiles,),
        in_specs=[idx(lambda i: (i,)), idx(lambda i: (jnp.minimum(i + 1, n_tiles - 1),)), row,
                  pl.BlockSpec((tm, ROUTE_LANES), lambda i: (i, 0)), pl.BlockSpec(memory_space=pl.ANY),
                  _full((1, d)), _full((1, d))],
        out_specs=row,
        out_shape=jax.ShapeDtypeStruct((n, d), _F32),
        scratch_shapes=[pltpu.VMEM((2, TOP_K, tm, d), _F32), pltpu.SemaphoreType.DMA((2, TOP_K))],
        compiler_params=_cparams("arbitrary"),
        name="moe_combine",
    )(dest, dest, x2, info, ybuf, g[None], b[None])


def _moe(x2, w_r, b_r, g, b, *, tm, cast=None, experts=None):
    n, d = x2.shape
    n_assign = n * TOP_K
    blk = min(MOE_BLOCK, max(8, -(-n_assign // (8 * N_EXPERTS)) * 8))
    n_blocks = -(-(n_assign + N_EXPERTS * (blk - 1)) // blk)
    if experts is None:
        w_e_in, w_e_out, layer = cast
        info, cnt, w_in_b = _router(x2, w_r, b_r, tm=tm, cast=(w_e_in, layer))
    else:
        info, cnt = _router(x2, w_r, b_r, tm=tm)
    dest, blk_e, nblk, fill = _route_plan(info, cnt, blk=blk, n_blocks=n_blocks)
    if experts is None:
        xbuf, w_out_b = _dispatch(x2, dest, fill, rows=n_blocks * blk, tm=tm, cast=(w_e_out, layer))
        experts = (w_in_b, w_out_b)
    else:
        (xbuf,) = _dispatch(x2, dest, fill, rows=n_blocks * blk, tm=tm)
    ybuf = _moe_experts(xbuf, blk_e, nblk, *experts, blk=blk)
    return _combine(x2, info, dest, ybuf, g, b, tm=tm), tuple(experts)


def kernel(x_prompt, x_sample, cache_b_k, cache_b_v, state_c_conv, ln_g, ln_b, w_a_in, b_a_in, ln_a_g, ln_a_b,
           w_a_s, b_a_s, w_a_out, w_b_qkv, b_b_qkv, b_sink, w_b_out, w_c_in, b_c_in, w_c_dw, b_c_dw, ln_c_g,
           ln_c_b, w_c_out, w_f_in, w_f_out, w_r, b_r, w_e_in, w_e_out):
    bsz, t, d = x_prompt.shape
    dbs, s, _ = x_sample.shape
    n_p, n_s = bsz * t, dbs * s
    xp = x_prompt.reshape(n_p, d)
    xs = x_sample.reshape(n_s, d)
    a_v_s, b_k_p, b_v_p, b_k_s, b_v_s, c_p, c_s = [], [], [], [], [], [], []
    w_f_in_b, w_f_out_b = w_f_in.astype(_BF16), w_f_out.astype(_BF16)
    for i in range(DEPTH):
        kind, j = i % N_MIXERS, i // N_MIXERS
        g0, b0 = ln_g[i, 0], ln_b[i, 0]
        if kind == 0:
            args = (w_a_in[j], b_a_in[j], ln_a_g[j], ln_a_b[j], w_a_s[j], b_a_s[j], w_a_out[j], g0, b0)
            (xp,) = _gmlp(xp, *args, chunk=A_CHUNK, tm=ROW_TILE, emit_v=False)
            xs, v_rows = _gmlp(xs, *args, chunk=s, tm=n_s, emit_v=True)
            a_v_s.append(v_rows.reshape(dbs, s, d))
        elif kind == 1:
            args = (w_b_qkv[j], b_b_qkv[j], b_sink[j], w_b_out[j], g0, b0)
            xp3, kp, vp = _swa_prompt(xp.reshape(bsz, t, d), *args, tm=ROW_TILE)
            xs3, kn, vn = _swa_sample(xs.reshape(dbs, s, d), cache_b_k[j], cache_b_v[j], *args)
            xp, xs = xp3.reshape(n_p, d), xs3.reshape(n_s, d)
            b_k_p.append(kp)
            b_v_p.append(vp)
            b_k_s.append(kn)
            b_v_s.append(vn)
        else:
            args = (w_c_in[j], b_c_in[j], w_c_dw[j], b_c_dw[j], ln_c_g[j], ln_c_b[j], w_c_out[j], g0, b0)
            zero_state = jnp.zeros((bsz, CONV_WIDTH - 1, d), _F32)
            xp3, cp = _conv(xp.reshape(bsz, t, d), zero_state, *args, tm=ROW_TILE)
            xs3, cs = _conv(xs.reshape(dbs, s, d), state_c_conv[j], *args, tm=s)
            xp, xs = xp3.reshape(n_p, d), xs3.reshape(n_s, d)
            c_p.append(cp)
            c_s.append(cs)
        f = i // 2
        g1, b1 = ln_g[i, 1], ln_b[i, 1]
        if i % 2 == 0:
            xp = _ffn(xp, w_f_in_b, w_f_out_b, g1, b1, layer=f, tm=FFN_ROW_TILE)
            xs = _ffn(xs, w_f_in_b, w_f_out_b, g1, b1, layer=f, tm=n_s)
        else:
            xp, experts = _moe(xp, w_r[f], b_r[f], g1, b1, tm=ROW_TILE, cast=(w_e_in, w_e_out, f))
            xs, _ = _moe(xs, w_r[f], b_r[f], g1, b1, tm=n_s, experts=experts)
    return (xp.reshape(bsz, t, d), xs.reshape(dbs, s, d), jnp.stack(a_v_s), jnp.stack(b_k_p),
            jnp.stack(b_v_p), jnp.stack(b_k_s), jnp.stack(b_v_s), jnp.stack(c_p), jnp.stack(c_s))
```

```python
import functools

import jax
import jax.numpy as jnp
import numpy as np
from jax import lax
from jax.experimental import pallas as pl
from jax.experimental.pallas import tpu as pltpu

D_MODEL = 1024
DEPTH = 4
PAST_LEN = 1024
CHUNK = 64
N_MIXERS = 3
ALPHA = (2 * DEPTH) ** 0.25
LN_EPS = 1e-5
A_CHUNK = 128
A_GROUPS = 4
HEAD_DIM = 64
B_Q_HEADS = D_MODEL // HEAD_DIM
B_KV_HEADS = 4
B_GROUP = B_Q_HEADS // B_KV_HEADS
WINDOW = 128
BAND_CHUNKS = WINDOW // CHUNK
ROT_DIM = HEAD_DIM // 4
ROPE_THETA = 500000.0
CONV_WIDTH = 31
N_EXPERTS = 8
TOP_K = 2
MOE_BLOCK = 512

VMEM_LIMIT_BYTES = 56 * 1024 * 1024
ROW_TILE = 512
FFN_ROW_TILE = 1024
FF_SUB = 256
CONV_HALO = 32
GROUP_LANES = B_GROUP * HEAD_DIM
ROPE_LANES = 128

_BF16 = jnp.bfloat16
_F32 = jnp.float32


def _cparams(*sem):
    return pltpu.CompilerParams(dimension_semantics=sem, vmem_limit_bytes=VMEM_LIMIT_BYTES)


def _dot(a, b):
    return jnp.dot(a, b, preferred_element_type=_F32)


def _layer_norm(x, g, b):
    mu = jnp.mean(x, axis=-1, keepdims=True)
    xc = x - mu
    var = jnp.mean(xc * xc, axis=-1, keepdims=True)
    return xc * lax.rsqrt(var + LN_EPS) * g + b


def _sigmoid(x):
    return 1.0 / (1.0 + jnp.exp(-x))


def _full(shape):
    nd = len(shape)
    return pl.BlockSpec(shape, lambda *_: (0,) * nd)


def _gmlp_kernel(x_ref, w_in_ref, b_in_ref, g_a_ref, b_a_ref, w_s_ref, bz_ref, w_out_ref,
                 g_ref, b_ref, *out_refs, chunk, emit_v):
    o_ref = out_refs[0]
    x = x_ref[...]
    tm = x.shape[0]
    h = _dot(x.astype(_BF16), w_in_ref[...]) + b_in_ref[...]
    h = jax.nn.gelu(h)
    u = h[:, :D_MODEL]
    v = _layer_norm(h[:, D_MODEL:], g_a_ref[...], b_a_ref[...])
    if emit_v:
        out_refs[1][...] = v
    vb = v.astype(_BF16)
    ri = lax.broadcasted_iota(jnp.int32, (chunk, chunk), 0) // CHUNK
    ci = lax.broadcasted_iota(jnp.int32, (chunk, chunk), 1) // CHUNK
    gw = D_MODEL // A_GROUPS
    ws = [jnp.where(ci <= ri, w_s_ref[g], 0.0).astype(_BF16) for g in range(A_GROUPS)]
    rows = []
    for n in range(tm // chunk):
        r0 = n * chunk
        cols = [_dot(ws[g], vb[r0:r0 + chunk, g * gw:(g + 1) * gw]) for g in range(A_GROUPS)]
        rows.append(jnp.concatenate(cols, axis=1) + bz_ref[...])
    z = jnp.concatenate(rows, axis=0) if len(rows) > 1 else rows[0]
    y = _dot((u * z).astype(_BF16), w_out_ref[...])
    o_ref[...] = _layer_norm(ALPHA * x + y, g_ref[...], b_ref[...])


def _gmlp(x2, w_in, b_in, g_a, b_a, w_s, b_s, w_out, g, b, *, chunk, tm, emit_v):
    n = x2.shape[0]
    d = D_MODEL
    gw = d // A_GROUPS
    bz = jnp.repeat(b_s[:, :chunk].T, gw, axis=1)
    out_shape = [jax.ShapeDtypeStruct((n, d), _F32)]
    out_specs = [pl.BlockSpec((tm, d), lambda i: (i, 0))]
    if emit_v:
        out_shape.append(jax.ShapeDtypeStruct((n, d), _F32))
        out_specs.append(pl.BlockSpec((tm, d), lambda i: (i, 0)))
    res = pl.pallas_call(
        functools.partial(_gmlp_kernel, chunk=chunk, emit_v=emit_v),
        grid=(n // tm,),
        in_specs=[
            pl.BlockSpec((tm, d), lambda i: (i, 0)),
            _full((d, 2 * d)), _full((1, 2 * d)), _full((1, d)), _full((1, d)),
            _full((A_GROUPS, chunk, chunk)), _full((chunk, d)), _full((d, d)),
            _full((1, d)), _full((1, d)),
        ],
        out_specs=out_specs,
        out_shape=out_shape,
        compiler_params=_cparams("arbitrary"),
        name="gmlp_mixer",
    )(x2, w_in.astype(_BF16), b_in[None], g_a[None], b_a[None], w_s[:, :chunk, :chunk], bz,
      w_out.astype(_BF16), g[None], b[None])
    return res


def _rope_tables(pos):
    half = ROT_DIM // 2
    inv = ROPE_THETA ** (-np.arange(0, ROT_DIM, 2, dtype=np.float32) / ROT_DIM)
    lane = np.arange(ROPE_LANES) % HEAD_DIM
    inv_lane = jnp.asarray(inv[lane % half], _F32)[None, :]
    lo = jnp.asarray(lane < half)[None, :]
    hi = jnp.asarray((lane >= half) & (lane < ROT_DIM))[None, :]
    ang = pos.astype(_F32)[:, None] * inv_lane
    cos, sin = jnp.cos(ang), jnp.sin(ang)
    c = jnp.where(lo | hi, cos, 1.0)
    s_up = jnp.where(hi, sin, 0.0)
    s_dn = jnp.where(lo, -sin, 0.0)
    return c, s_up, s_dn


def _rotary(x, c, s_up, s_dn):
    half = ROT_DIM // 2
    slabs = []
    for i in range(x.shape[1] // ROPE_LANES):
        xs = x[:, i * ROPE_LANES:(i + 1) * ROPE_LANES]
        slabs.append(xs * c + pltpu.roll(xs, half, axis=1) * s_up
                     + pltpu.roll(xs, ROPE_LANES - half, axis=1) * s_dn)
    return jnp.concatenate(slabs, axis=1)


def _expand_kv_weights(w_qkv, b_qkv):
    nq = B_Q_HEADS * HEAD_DIM
    nk = B_KV_HEADS * HEAD_DIM
    scale = HEAD_DIM ** -0.5
    assert float(np.log2(scale)).is_integer()

    def rep(a):
        lead = a.shape[:-1]
        a = a.reshape(lead + (B_KV_HEADS, 1, HEAD_DIM))
        a = jnp.broadcast_to(a, lead + (B_KV_HEADS, B_GROUP, HEAD_DIM))
        return a.reshape(lead + (nq,))

    w = jnp.concatenate([w_qkv[:, :nq] * scale, rep(w_qkv[:, nq:nq + nk]), rep(w_qkv[:, nq + nk:])], axis=1)
    bb = jnp.concatenate([b_qkv[:nq] * scale, rep(b_qkv[nq:nq + nk]), rep(b_qkv[nq + nk:])])
    return w, bb


def _attend(q, kwin, vwin, valid, sink_col):
    nq = q.shape[0]
    lane_head = lax.broadcasted_iota(jnp.int32, (1, GROUP_LANES), 1) // HEAD_DIM
    masks = [(lane_head == g).astype(_F32) for g in range(B_GROUP)]
    qs = jnp.concatenate([q * masks[g] for g in range(B_GROUP)], axis=0).astype(_BF16)
    s = lax.dot_general(qs, kwin, (((1,), (1,)), ((), ())), preferred_element_type=_F32)
    if valid is not None:
        s = jnp.where(valid, s, -1e30)
    m = jnp.maximum(jnp.max(s, axis=-1, keepdims=True), sink_col)
    p = jnp.exp(s - m)
    p = p / (jnp.sum(p, axis=-1, keepdims=True) + jnp.exp(sink_col - m))
    o = _dot(p.astype(_BF16), vwin)
    out = o[0:nq] * masks[0]
    for g in range(1, B_GROUP):
        out = out + o[g * nq:(g + 1) * nq] * masks[g]
    return out


def _sink_cols(sink_ref, hk, nq):
    return jnp.concatenate(
        [jnp.full((nq, 1), sink_ref[hk * B_GROUP + g], _F32) for g in range(B_GROUP)], axis=0)


SCORE_LANES = 256
SOFTMAX_ROWS = 512


def _swa_prompt_kernel(sink_ref, x_ref, w_ref, bq_ref, c_ref, su_ref, sd_ref, w_out_ref, g_ref, b_ref,
                       o_ref, kc_ref, vc_ref, k_ext, v_ext, s_ref, p_ref):
    t = pl.program_id(1)
    tm = x_ref.shape[0]
    d = D_MODEL
    halo = BAND_CHUNKS * CHUNK
    nkeys = (BAND_CHUNKS + 1) * CHUNK
    n_chunks = tm // CHUNK
    grp = B_GROUP * CHUNK

    @pl.when(t == 0)
    def _():
        k_ext[0:halo, :] = jnp.zeros((halo, d), _BF16)
        v_ext[0:halo, :] = jnp.zeros((halo, d), _BF16)
        k_ext[halo + tm:, :] = jnp.zeros((k_ext.shape[0] - halo - tm, d), _BF16)

    x = x_ref[...]
    qkv = _dot(x.astype(_BF16), w_ref[...]) + bq_ref[...]
    qk = _rotary(qkv[:, :2 * d], c_ref[...], su_ref[...], sd_ref[...])
    q = qk[:, :d]
    k = qk[:, d:]
    v = qkv[:, 2 * d:]
    k_ext[halo:halo + tm, :] = k.astype(_BF16)
    v_ext[halo:halo + tm, :] = v.astype(_BF16)
    kc_ref[...] = k[tm - WINDOW:, :]
    vc_ref[...] = v[tm - WINDOW:, :]

    lane_head = lax.broadcasted_iota(jnp.int32, (1, GROUP_LANES), 1) // HEAD_DIM
    masks = [(lane_head == g).astype(_F32) for g in range(B_GROUP)]
    key_lane = lax.broadcasted_iota(jnp.int32, (grp, SCORE_LANES), 1)
    tails = [jnp.where(key_lane == nkeys, _sink_cols(sink_ref, hk, CHUNK), -1e30) for hk in range(B_KV_HEADS)]

    for n in range(n_chunks):
        r0 = n * CHUNK
        if n < BAND_CHUNKS:
            first = jnp.maximum((BAND_CHUNKS - n - t * n_chunks) * CHUNK, 0)
            keep = (lax.bitcast_convert_type(key_lane - first, jnp.uint32)
                    < lax.bitcast_convert_type(nkeys - first, jnp.uint32))
        else:
            keep = key_lane < nkeys
        for hk in range(B_KV_HEADS):
            l0 = hk * GROUP_LANES
            qn = q[r0:r0 + CHUNK, l0:l0 + GROUP_LANES]
            qs = jnp.concatenate([qn * masks[g] for g in range(B_GROUP)], axis=0).astype(_BF16)
            s = lax.dot_general(qs, k_ext[r0:r0 + SCORE_LANES, l0:l0 + GROUP_LANES],
                                (((1,), (1,)), ((), ())), preferred_element_type=_F32)
            i0 = (n * B_KV_HEADS + hk) * grp
            s_ref[i0:i0 + grp, :] = jnp.where(keep, s, tails[hk])

    for i in range(s_ref.shape[0] // SOFTMAX_ROWS):
        s = s_ref[i * SOFTMAX_ROWS:(i + 1) * SOFTMAX_ROWS, :]
        p = jnp.exp(s - jnp.max(s, axis=-1, keepdims=True))
        p_ref[i * SOFTMAX_ROWS:(i + 1) * SOFTMAX_ROWS, :] = (p / jnp.sum(p, axis=-1, keepdims=True)).astype(_BF16)

    rows = []
    for n in range(n_chunks):
        r0 = n * CHUNK
        cols = []
        for hk in range(B_KV_HEADS):
            l0 = hk * GROUP_LANES
            i0 = (n * B_KV_HEADS + hk) * grp
            o = _dot(p_ref[i0:i0 + grp, 0:nkeys], v_ext[r0:r0 + nkeys, l0:l0 + GROUP_LANES])
            out = o[0:CHUNK] * masks[0]
            for g in range(1, B_GROUP):
                out = out + o[g * CHUNK:(g + 1) * CHUNK] * masks[g]
            cols.append(out)
        rows.append(jnp.concatenate(cols, axis=1))
    att = jnp.concatenate(rows, axis=0)
    y = _dot(att.astype(_BF16), w_out_ref[...])
    o_ref[...] = _layer_norm(ALPHA * x + y, g_ref[...], b_ref[...])
    k_ext[0:halo, :] = k_ext[tm:tm + halo, :]
    v_ext[0:halo, :] = v_ext[tm:tm + halo, :]


def _unexpand(a):
    lead = a.shape[:-1]
    return a.reshape(lead + (B_KV_HEADS, B_GROUP, HEAD_DIM))[..., 0, :]


def _swa_prompt(x3, w_qkv, b_qkv, sink, w_out, g, b, *, tm):
    bsz, t, d = x3.shape
    w_exp, b_exp = _expand_kv_weights(w_qkv, b_qkv)
    c, su, sd = _rope_tables(jnp.arange(t))
    tab = pl.BlockSpec((tm, 2 * HEAD_DIM), lambda bi, ti, *_: (ti, 0))
    xspec = pl.BlockSpec((None, tm, d), lambda bi, ti, *_: (bi, ti, 0))
    cspec = pl.BlockSpec((None, WINDOW, d), lambda bi, ti, *_: (bi, 0, 0))
    halo = BAND_CHUNKS * CHUNK
    score_rows = (tm // CHUNK) * B_KV_HEADS * B_GROUP * CHUNK
    assert score_rows % SOFTMAX_ROWS == 0
    one = pl.Buffered(1)
    y, kc, vc = pl.pallas_call(
        _swa_prompt_kernel,
        grid_spec=pltpu.PrefetchScalarGridSpec(
            num_scalar_prefetch=1,
            grid=(bsz, t // tm),
            in_specs=[xspec, pl.BlockSpec((d, 3 * d), lambda *_: (0, 0), pipeline_mode=one),
                      _full((1, 3 * d)), tab, tab, tab,
                      pl.BlockSpec((d, d), lambda *_: (0, 0), pipeline_mode=one),
                      _full((1, d)), _full((1, d))],
            out_specs=[xspec, cspec, cspec],
            scratch_shapes=[
                pltpu.VMEM((halo + tm + SCORE_LANES - (halo + CHUNK), d), _BF16),
                pltpu.VMEM((halo + tm, d), _BF16),
                pltpu.VMEM((score_rows, SCORE_LANES), _F32),
                pltpu.VMEM((score_rows, SCORE_LANES), _BF16)],
        ),
        out_shape=[jax.ShapeDtypeStruct((bsz, t, d), _F32),
                   jax.ShapeDtypeStruct((bsz, WINDOW, d), _F32),
                   jax.ShapeDtypeStruct((bsz, WINDOW, d), _F32)],
        compiler_params=_cparams("arbitrary", "arbitrary"),
        name="swa_prompt_mixer",
    )(sink, x3, w_exp.astype(_BF16), b_exp[None], c, su, sd, w_out.astype(_BF16), g[None], b[None])
    return y, _unexpand(kc), _unexpand(vc)


def _swa_sample_kernel(sink_ref, x_ref, ck_ref, cv_ref, valid_ref, w_ref, bq_ref, c_ref, su_ref, sd_ref,
                       w_out_ref, g_ref, b_ref, o_ref, kn_ref, vn_ref, *, bsz, s):
    d = D_MODEL
    x = x_ref[...]
    qkv = _dot(x.astype(_BF16), w_ref[...]) + bq_ref[...]
    qk = _rotary(qkv[:, :2 * d], c_ref[...], su_ref[...], sd_ref[...])
    q = qk[:, :d]
    k = qk[:, d:]
    v = qkv[:, 2 * d:]
    kn_ref[...] = k
    vn_ref[...] = v
    kb = k.astype(_BF16)
    vb = v.astype(_BF16)
    valid = jnp.concatenate([valid_ref[...] > 0] * B_GROUP, axis=0)
    rows = []
    for bi in range(bsz):
        r0 = bi * s
        kk = jnp.concatenate([ck_ref[bi], kb[r0:r0 + s]], axis=0)
        vv = jnp.concatenate([cv_ref[bi], vb[r0:r0 + s]], axis=0)
        cols = []
        for hk in range(B_KV_HEADS):
            l0 = hk * GROUP_LANES
            cols.append(_attend(q[r0:r0 + s, l0:l0 + GROUP_LANES], kk[:, l0:l0 + GROUP_LANES],
                                vv[:, l0:l0 + GROUP_LANES], valid, _sink_cols(sink_ref, hk, s)))
        rows.append(jnp.concatenate(cols, axis=1))
    att = jnp.concatenate(rows, axis=0)
    y = _dot(att.astype(_BF16), w_out_ref[...])
    o_ref[...] = _layer_norm(ALPHA * x + y, g_ref[...], b_ref[...])


def _swa_sample(x3, ck, cv, w_qkv, b_qkv, sink, w_out, g, b):
    bsz, s, d = x3.shape
    nc = ck.shape[1]
    w_exp, b_exp = _expand_kv_weights(w_qkv, b_qkv)
    qpos = PAST_LEN + np.arange(s)
    kpos = np.concatenate([PAST_LEN - nc + np.arange(nc), qpos])
    qc, kc = qpos // CHUNK, kpos // CHUNK
    valid = ((kc[None, :] <= qc[:, None]) & (kc[None, :] >= qc[:, None] - BAND_CHUNKS)).astype(np.int32)
    c, su, sd = _rope_tables(jnp.tile(jnp.asarray(qpos), bsz))

    def expand_cache(a):
        a = jnp.broadcast_to(a[:, :, :, None, :], (bsz, nc, B_KV_HEADS, B_GROUP, HEAD_DIM))
        return a.reshape(bsz, nc, d).astype(_BF16)

    n = bsz * s
    vm = pl.BlockSpec(memory_space=pltpu.VMEM)
    y, kn, vn = pl.pallas_call(
        functools.partial(_swa_sample_kernel, bsz=bsz, s=s),
        in_specs=[pl.BlockSpec(memory_space=pltpu.SMEM)] + [vm] * 12,
        out_specs=[vm, vm, vm],
        out_shape=[jax.ShapeDtypeStruct((n, d), _F32)] * 3,
        compiler_params=pltpu.CompilerParams(vmem_limit_bytes=VMEM_LIMIT_BYTES),
        name="swa_sample_mixer",
    )(sink, x3.reshape(n, d), expand_cache(ck), expand_cache(cv), jnp.asarray(valid),
      w_exp.astype(_BF16), b_exp[None], c, su, sd, w_out.astype(_BF16), g[None], b[None])
    return (y.reshape(bsz, s, d), _unexpand(kn.reshape(bsz, s, d)), _unexpand(vn.reshape(bsz, s, d)))


def _conv_kernel(x_ref, st_ref, w_in_ref, b_in_ref, w_dw_ref, b_dw_ref, g_c_ref, b_c_ref, w_out_ref,
                 g_ref, b_ref, o_ref, st_out_ref, h_ext):
    t = pl.program_id(1)
    tm = x_ref.shape[0]
    d = D_MODEL

    @pl.when(t == 0)
    def _():
        h_ext[0:CONV_HALO, :] = st_ref[...]

    x = x_ref[...]
    ag = _dot(x.astype(_BF16), w_in_ref[...]) + b_in_ref[...]
    h = ag[:, :d] * _sigmoid(ag[:, d:])
    h_ext[CONV_HALO:CONV_HALO + tm, :] = h
    base = CONV_HALO - (CONV_WIDTH - 1)
    acc = None
    for sub in range(8):
        offs = [o for o in range(base, base + CONV_WIDTH) if o % 8 == sub]
        if not offs:
            continue
        rows = tm + (8 if sub else 0)
        part = None
        for o in offs:
            a0 = o - sub
            term = h_ext[a0:a0 + rows, :] * w_dw_ref[o - base:o - base + 1, :]
            part = term if part is None else part + term
        part = part[sub:sub + tm, :]
        acc = part if acc is None else acc + part
    y = acc + b_dw_ref[...]
    y = _layer_norm(y, g_c_ref[...], b_c_ref[...])
    y = y * _sigmoid(y)
    y = _dot(y.astype(_BF16), w_out_ref[...])
    o_ref[...] = _layer_norm(ALPHA * x + y, g_ref[...], b_ref[...])
    new_hist = h_ext[tm:tm + CONV_HALO, :]
    st_out_ref[...] = new_hist
    h_ext[0:CONV_HALO, :] = new_hist


def _conv(x3, state, w_in, b_in, w_dw, b_dw, g_c, b_c, w_out, g, b, *, tm):
    bsz, t, d = x3.shape
    st = jnp.pad(state, ((0, 0), (CONV_HALO - (CONV_WIDTH - 1), 0), (0, 0)))
    xspec = pl.BlockSpec((None, tm, d), lambda bi, ti: (bi, ti, 0))
    sspec = pl.BlockSpec((None, CONV_HALO, d), lambda bi, ti: (bi, 0, 0))
    y, st_new = pl.pallas_call(
        _conv_kernel,
        grid=(bsz, t // tm),
        in_specs=[xspec, sspec, _full((d, 2 * d)), _full((1, 2 * d)), _full((CONV_WIDTH, d)), _full((1, d)),
                  _full((1, d)), _full((1, d)), _full((d, d)), _full((1, d)), _full((1, d))],
        out_specs=[xspec, sspec],
        out_shape=[jax.ShapeDtypeStruct((bsz, t, d), _F32),
                   jax.ShapeDtypeStruct((bsz, CONV_HALO, d), _F32)],
        scratch_shapes=[pltpu.VMEM((tm + CONV_HALO, d), _F32)],
        compiler_params=_cparams("arbitrary", "arbitrary"),
        name="conv_mixer",
    )(x3, st, w_in.astype(_BF16), b_in[None], w_dw, b_dw[None], g_c[None], b_c[None],
      w_out.astype(_BF16), g[None], b[None])
    return y, st_new[:, CONV_HALO - (CONV_WIDTH - 1):]


def _swiglu_partial(xb, wg_ref, wu_ref, wo_ref, width):
    acc = None
    for c in range(width // FF_SUB):
        sl = slice(c * FF_SUB, (c + 1) * FF_SUB)
        gt = _dot(xb, wg_ref[:, sl])
        up = _dot(xb, wu_ref[:, sl])
        hh = (gt * _sigmoid(gt) * up).astype(_BF16)
        part = _dot(hh, wo_ref[sl, :])
        acc = part if acc is None else acc + part
    return acc


def _ffn_kernel(x_ref, wg_ref, wu_ref, wo_ref, g_ref, b_ref, o_ref, *, d_ff):
    x = x_ref[...]
    f = _swiglu_partial(x.astype(_BF16), wg_ref, wu_ref, wo_ref, d_ff)
    o_ref[...] = _layer_norm(ALPHA * x + f, g_ref[...], b_ref[...])


def _ffn(x2, w_in, w_out, g, b, *, layer, tm):
    n, d = x2.shape
    d_ff = w_out.shape[1]
    one = pl.Buffered(1)
    return pl.pallas_call(
        functools.partial(_ffn_kernel, d_ff=d_ff),
        grid=(n // tm,),
        in_specs=[
            pl.BlockSpec((tm, d), lambda i: (i, 0)),
            pl.BlockSpec((None, d, d_ff), lambda i: (layer, 0, 0), pipeline_mode=one),
            pl.BlockSpec((None, d, d_ff), lambda i: (layer, 0, 1), pipeline_mode=one),
            pl.BlockSpec((None, d_ff, d), lambda i: (layer, 0, 0), pipeline_mode=one),
            _full((1, d)), _full((1, d)),
        ],
        out_specs=pl.BlockSpec((tm, d), lambda i: (i, 0)),
        out_shape=jax.ShapeDtypeStruct((n, d), _F32),
        compiler_params=_cparams("arbitrary"),
        name="dense_swiglu",
    )(x2, w_in, w_in, w_out, g[None], b[None])


def _moe_kernel(blk_e_ref, nblk_ref, x_ref, wg_ref, wu_ref, wo_ref, o_ref, acc_ref, *, width):
    bi = pl.program_id(0)
    j = pl.program_id(1)
    last = pl.num_programs(1) - 1
    live = bi < nblk_ref[0]

    @pl.when(live)
    def _():
        part = _swiglu_partial(x_ref[...].astype(_BF16), wg_ref, wu_ref, wo_ref, width)

        @pl.when(j == 0)
        def _():
            acc_ref[...] = part

        @pl.when((j > 0) & (j < last))
        def _():
            acc_ref[...] += part

        @pl.when(j == last)
        def _():
            o_ref[...] = acc_ref[...] + part

    @pl.when(jnp.logical_not(live) & (j == last))
    def _():
        o_ref[...] = jnp.zeros(o_ref.shape, o_ref.dtype)


def _moe_experts(xbuf, blk_e, nblk, w_e_in, w_e_out, *, blk):
    rows, d = xbuf.shape
    n_blocks = rows // blk
    d_ff = w_e_out.shape[1]
    splits = 2
    width = d_ff // splits

    def wj(i, j, nb):
        return jnp.where(i < nb[0], j, splits - 1)

    return pl.pallas_call(
        functools.partial(_moe_kernel, width=width),
        grid_spec=pltpu.PrefetchScalarGridSpec(
            num_scalar_prefetch=2,
            grid=(n_blocks, splits),
            in_specs=[
                pl.BlockSpec((blk, d), lambda i, j, e, nb: (i, 0)),
                pl.BlockSpec((None, d, width), lambda i, j, e, nb: (e[i], 0, wj(i, j, nb))),
                pl.BlockSpec((None, d, width), lambda i, j, e, nb: (e[i], 0, splits + wj(i, j, nb))),
                pl.BlockSpec((None, width, d), lambda i, j, e, nb: (e[i], wj(i, j, nb), 0)),
            ],
            out_specs=pl.BlockSpec((blk, d), lambda i, j, e, nb: (i, 0)),
            scratch_shapes=[pltpu.VMEM((blk, d), _F32)],
        ),
        out_shape=jax.ShapeDtypeStruct((rows, d), _F32),
        compiler_params=_cparams("arbitrary", "arbitrary"),
        name="moe_experts",
    )(blk_e, nblk, xbuf, w_e_in, w_e_in, w_e_out)


ROUTE_LANES = 128


def _router_kernel(x_ref, wh_ref, wl_ref, br_ref, *rest, cast_weights):
    if cast_weights:
        w_ref, info_ref, cnt_ref, w_b_ref, base_ref = rest
        w_b_ref[...] = w_ref[...].astype(_BF16)
    else:
        info_ref, cnt_ref, base_ref = rest

    @pl.when(pl.program_id(0) == 0)
    def _():
        base_ref[...] = jnp.zeros(base_ref.shape, _F32)

    x = x_ref[...]
    tm = x.shape[0]
    xh = x.astype(_BF16)
    xl = (x - xh.astype(_F32)).astype(_BF16)
    logits = _dot(xh, wh_ref[...]) + (_dot(xh, wl_ref[...]) + _dot(xl, wh_ref[...])) + br_ref[...]
    lane = lax.broadcasted_iota(jnp.int32, logits.shape, 1)
    neg = -jnp.inf
    logits = jnp.where(lane < N_EXPERTS, logits, neg)
    m1 = jnp.max(logits, axis=-1, keepdims=True)
    i1 = jnp.min(jnp.where(logits == m1, lane, ROUTE_LANES), axis=-1, keepdims=True)
    rest = jnp.where(lane == i1, neg, logits)
    m2 = jnp.max(rest, axis=-1, keepdims=True)
    i2 = jnp.min(jnp.where(rest == m2, lane, ROUTE_LANES), axis=-1, keepdims=True)
    ex = jnp.exp(m2 - m1)
    g1 = 1.0 / (1.0 + ex)
    g2 = ex / (1.0 + ex)
    oh1 = (lane == i1).astype(_F32)
    oh2 = (lane == i2).astype(_F32)
    oh = oh1 + oh2
    tri = (lax.broadcasted_iota(jnp.int32, (tm, tm), 0) > lax.broadcasted_iota(jnp.int32, (tm, tm), 1))
    before = _dot(tri.astype(_BF16), oh.astype(_BF16)) + base_ref[...]
    r1 = jnp.sum(before * oh1, axis=-1, keepdims=True)
    r2 = jnp.sum(before * oh2, axis=-1, keepdims=True)
    base_ref[...] += jnp.sum(oh, axis=0, keepdims=True)
    cnt_ref[...] = base_ref[...]
    cols = (i1.astype(_F32), i2.astype(_F32), r1, r2, g1, g2)
    info = jnp.zeros(logits.shape, _F32)
    for c, val in enumerate(cols):
        info = jnp.where(lane == c, val, info)
    info_ref[...] = info


def _cast_specs(w, layer, n_tiles):
    n_e = w.shape[1]
    per_e = n_tiles // n_e
    rows, cols = w.shape[2] // per_e, w.shape[3]
    assert n_tiles == per_e * n_e and w.shape[2] == rows * per_e and rows % 16 == 0
    w_spec = pl.BlockSpec((None, None, rows, cols), lambda i, *_: (layer, i // per_e, i % per_e, 0))
    wb_spec = pl.BlockSpec((None, rows, cols), lambda i, *_: (i // per_e, i % per_e, 0))
    return w_spec, wb_spec, jax.ShapeDtypeStruct(w.shape[1:], _BF16)


def _router(x2, w_r, b_r, *, tm, cast=None):
    n, d = x2.shape
    n_tiles = n // tm
    wp = jnp.pad(w_r, ((0, 0), (0, ROUTE_LANES - N_EXPERTS)))
    wh = wp.astype(_BF16)
    wl = (wp - wh.astype(_F32)).astype(_BF16)
    bp = jnp.pad(b_r, (0, ROUTE_LANES - N_EXPERTS))[None]
    in_specs = [pl.BlockSpec((tm, d), lambda i: (i, 0)), _full((d, ROUTE_LANES)), _full((d, ROUTE_LANES)),
                _full((1, ROUTE_LANES))]
    out_specs = [pl.BlockSpec((tm, ROUTE_LANES), lambda i: (i, 0)), _full((1, ROUTE_LANES))]
    out_shape = [jax.ShapeDtypeStruct((n, ROUTE_LANES), _F32), jax.ShapeDtypeStruct((1, ROUTE_LANES), _F32)]
    args = [x2, wh, wl, bp]
    if cast is not None:
        w_spec, wb_spec, wb_shape = _cast_specs(*cast, n_tiles)
        in_specs.append(w_spec)
        out_specs.append(wb_spec)
        out_shape.append(wb_shape)
        args.append(cast[0])
    return pl.pallas_call(
        functools.partial(_router_kernel, cast_weights=cast is not None),
        grid=(n_tiles,),
        in_specs=in_specs,
        out_specs=out_specs,
        out_shape=out_shape,
        scratch_shapes=[pltpu.VMEM((1, ROUTE_LANES), _F32)],
        compiler_params=_cparams("arbitrary"),
        name="moe_router",
    )(*args)


def _route_plan(info, cnt, *, blk, n_blocks):
    e = info[:, 0:TOP_K].astype(jnp.int32)
    rank = info[:, TOP_K:2 * TOP_K].astype(jnp.int32)
    counts = cnt[0, :N_EXPERTS].astype(jnp.int32)
    padded = (counts + blk - 1) // blk * blk
    pad_end = jnp.cumsum(padded)
    pad_start = pad_end - padded
    dest = (pad_start[e] + rank).reshape(-1)
    nblk = pad_end[-1] // blk
    def owner(bounds, q):
        return jnp.minimum(jnp.sum(bounds[None, :] <= q[:, None], axis=1), N_EXPERTS - 1)

    blk_e = owner(pad_end, jnp.minimum(jnp.arange(n_blocks), nblk - 1) * blk).astype(jnp.int32)
    n_fill = n_blocks * blk - dest.shape[0]
    tail = padded - counts
    gaps = jnp.cumsum(tail)
    slot = jnp.arange(n_fill)
    ge = owner(gaps, slot)
    in_gap = pad_start[ge] + counts[ge] + slot - (gaps[ge] - tail[ge])
    fill = jnp.where(slot < gaps[-1], in_gap, pad_end[-1] + slot - gaps[-1]).astype(jnp.int32)
    return dest.astype(jnp.int32), blk_e, nblk.astype(jnp.int32)[None], fill


def _row_copy(src_ref, src_row, dst_ref, dst_row, sem):
    return pltpu.make_async_copy(src_ref.at[pl.ds(src_row, 1)], dst_ref.at[pl.ds(dst_row, 1)], sem)


def _dispatch_kernel(fill_ref, dest_ref, x_ref, *rest, n_fill, cast_weights):
    if cast_weights:
        w_ref, xbuf_ref, w_b_ref, zero_ref, sem, zsem = rest
        w_b_ref[...] = w_ref[...].astype(_BF16)
    else:
        xbuf_ref, zero_ref, sem, zsem = rest
    tm = x_ref.shape[0]
    zrows = zero_ref.shape[0]

    def issue(r, c):
        for k in range(TOP_K):
            _row_copy(x_ref, r, xbuf_ref, dest_ref[TOP_K * r + k], sem).start()
        return c

    lax.fori_loop(0, tm, issue, 0, unroll=8)

    @pl.when(pl.program_id(0) == pl.num_programs(0) - 1)
    def _():
        zero_ref[...] = jnp.zeros(zero_ref.shape, zero_ref.dtype)

        def zissue(r, c):
            _row_copy(zero_ref, 0, xbuf_ref, fill_ref[r], zsem).start()
            return c

        lax.fori_loop(0, n_fill, zissue, 0, unroll=8)

        def zwait(r, c):
            pltpu.make_async_copy(zero_ref, xbuf_ref.at[pl.ds(0, zrows)], zsem).wait()
            return c

        lax.fori_loop(0, n_fill // zrows, zwait, 0)

    for k in range(TOP_K):
        pltpu.make_async_copy(x_ref, xbuf_ref.at[pl.ds(0, tm)], sem).wait()


def _dispatch(x2, dest, fill, *, rows, tm, cast=None):
    n, d = x2.shape
    n_tiles = n // tm
    n_fill = fill.shape[0]
    zrows = 8
    assert n_fill % zrows == 0
    in_specs = [pl.BlockSpec((TOP_K * tm,), lambda i, f: (i,), memory_space=pltpu.SMEM),
                pl.BlockSpec((tm, d), lambda i, f: (i, 0))]
    out_specs = [pl.BlockSpec(memory_space=pl.ANY)]
    out_shape = [jax.ShapeDtypeStruct((rows, d), _F32)]
    args = [fill, dest, x2]
    if cast is not None:
        w_spec, wb_spec, wb_shape = _cast_specs(*cast, n_tiles)
        in_specs.append(w_spec)
        out_specs.append(wb_spec)
        out_shape.append(wb_shape)
        args.append(cast[0])
    return pl.pallas_call(
        functools.partial(_dispatch_kernel, n_fill=n_fill, cast_weights=cast is not None),
        grid_spec=pltpu.PrefetchScalarGridSpec(
            num_scalar_prefetch=1,
            grid=(n_tiles,),
            in_specs=in_specs,
            out_specs=out_specs,
            scratch_shapes=[pltpu.VMEM((zrows, d), _F32), pltpu.SemaphoreType.DMA, pltpu.SemaphoreType.DMA],
        ),
        out_shape=out_shape,
        compiler_params=_cparams("arbitrary"),
        name="moe_dispatch",
    )(*args)


def _combine_kernel(dest_ref, next_ref, x_ref, info_ref, ybuf_ref, g_ref, b_ref, o_ref, y_ref, sems):
    i = pl.program_id(0)
    tm = x_ref.shape[0]
    slot = i % 2
    nslot = 1 - slot

    def gather_tile(idx_ref, to_slot):
        def issue(r, c):
            for k in range(TOP_K):
                _row_copy(ybuf_ref, idx_ref[TOP_K * r + k], y_ref.at[to_slot, k], r, sems.at[to_slot, k]).start()
            return c

        lax.fori_loop(0, tm, issue, 0, unroll=8)

    def wait_tile(in_slot):
        for k in range(TOP_K):
            pltpu.make_async_copy(ybuf_ref.at[pl.ds(0, tm)], y_ref.at[in_slot, k], sems.at[in_slot, k]).wait()

    @pl.when(i == 0)
    def _():
        gather_tile(dest_ref, slot)

    gather_tile(next_ref, nslot)
    wait_tile(slot)
    info = info_ref[...]
    f = y_ref[slot, 0] * info[:, 2 * TOP_K:2 * TOP_K + 1]
    for k in range(1, TOP_K):
        f = f + y_ref[slot, k] * info[:, 2 * TOP_K + k:2 * TOP_K + k + 1]
    o_ref[...] = _layer_norm(ALPHA * x_ref[...] + f, g_ref[...], b_ref[...])

    @pl.when(i == pl.num_programs(0) - 1)
    def _():
        wait_tile(nslot)


def _combine(x2, info, dest, ybuf, g, b, *, tm):
    n, d = x2.shape
    n_tiles = n // tm
    row = pl.BlockSpec((tm, d), lambda i: (i, 0))
    idx = lambda f: pl.BlockSpec((TOP_K * tm,), f, memory_space=pltpu.SMEM)
    return pl.pallas_call(
        _combine_kernel,
        grid=(n_tiles,),
        in_specs=[idx(lambda i: (i,)), idx(lambda i: (jnp.minimum(i + 1, n_tiles - 1),)), row,
                  pl.BlockSpec((tm, ROUTE_LANES), lambda i: (i, 0)), pl.BlockSpec(memory_space=pl.ANY),
                  _full((1, d)), _full((1, d))],
        out_specs=row,
        out_shape=jax.ShapeDtypeStruct((n, d), _F32),
        scratch_shapes=[pltpu.VMEM((2, TOP_K, tm, d), _F32), pltpu.SemaphoreType.DMA((2, TOP_K))],
        compiler_params=_cparams("arbitrary"),
        name="moe_combine",
    )(dest, dest, x2, info, ybuf, g[None], b[None])


def _moe(x2, w_r, b_r, g, b, *, tm, cast=None, experts=None):
    n, d = x2.shape
    n_assign = n * TOP_K
    blk = min(MOE_BLOCK, max(8, -(-2 * n_assign // (8 * N_EXPERTS)) * 8))
    n_blocks = -(-(n_assign + N_EXPERTS * (blk - 1)) // blk)
    if experts is None:
        w_e_in, w_e_out, layer = cast
        info, cnt, w_in_b = _router(x2, w_r, b_r, tm=tm, cast=(w_e_in, layer))
    else:
        info, cnt = _router(x2, w_r, b_r, tm=tm)
    dest, blk_e, nblk, fill = _route_plan(info, cnt, blk=blk, n_blocks=n_blocks)
    if experts is None:
        xbuf, w_out_b = _dispatch(x2, dest, fill, rows=n_blocks * blk, tm=tm, cast=(w_e_out, layer))
        experts = (w_in_b, w_out_b)
    else:
        (xbuf,) = _dispatch(x2, dest, fill, rows=n_blocks * blk, tm=tm)
    ybuf = _moe_experts(xbuf, blk_e, nblk, *experts, blk=blk)
    return _combine(x2, info, dest, ybuf, g, b, tm=tm), tuple(experts)


def kernel(x_prompt, x_sample, cache_b_k, cache_b_v, state_c_conv, ln_g, ln_b, w_a_in, b_a_in, ln_a_g, ln_a_b,
           w_a_s, b_a_s, w_a_out, w_b_qkv, b_b_qkv, b_sink, w_b_out, w_c_in, b_c_in, w_c_dw, b_c_dw, ln_c_g,
           ln_c_b, w_c_out, w_f_in, w_f_out, w_r, b_r, w_e_in, w_e_out):
    bsz, t, d = x_prompt.shape
    dbs, s, _ = x_sample.shape
    n_p, n_s = bsz * t, dbs * s
    xp = x_prompt.reshape(n_p, d)
    xs = x_sample.reshape(n_s, d)
    a_v_s, b_k_p, b_v_p, b_k_s, b_v_s, c_p, c_s = [], [], [], [], [], [], []
    w_f_in_b, w_f_out_b = w_f_in.astype(_BF16), w_f_out.astype(_BF16)
    for i in range(DEPTH):
        kind, j = i % N_MIXERS, i // N_MIXERS
        g0, b0 = ln_g[i, 0], ln_b[i, 0]
        if kind == 0:
            args = (w_a_in[j], b_a_in[j], ln_a_g[j], ln_a_b[j], w_a_s[j], b_a_s[j], w_a_out[j], g0, b0)
            (xp,) = _gmlp(xp, *args, chunk=A_CHUNK, tm=ROW_TILE, emit_v=False)
            xs, v_rows = _gmlp(xs, *args, chunk=s, tm=n_s, emit_v=True)
            a_v_s.append(v_rows.reshape(dbs, s, d))
        elif kind == 1:
            args = (w_b_qkv[j], b_b_qkv[j], b_sink[j], w_b_out[j], g0, b0)
            xp3, kp, vp = _swa_prompt(xp.reshape(bsz, t, d), *args, tm=ROW_TILE)
            xs3, kn, vn = _swa_sample(xs.reshape(dbs, s, d), cache_b_k[j], cache_b_v[j], *args)
            xp, xs = xp3.reshape(n_p, d), xs3.reshape(n_s, d)
            b_k_p.append(kp)
            b_v_p.append(vp)
            b_k_s.append(kn)
            b_v_s.append(vn)
        else:
            args = (w_c_in[j], b_c_in[j], w_c_dw[j], b_c_dw[j], ln_c_g[j], ln_c_b[j], w_c_out[j], g0, b0)
            zero_state = jnp.zeros((bsz, CONV_WIDTH - 1, d), _F32)
            xp3, cp = _conv(xp.reshape(bsz, t, d), zero_state, *args, tm=ROW_TILE)
            xs3, cs = _conv(xs.reshape(dbs, s, d), state_c_conv[j], *args, tm=s)
            xp, xs = xp3.reshape(n_p, d), xs3.reshape(n_s, d)
            c_p.append(cp)
            c_s.append(cs)
        f = i // 2
        g1, b1 = ln_g[i, 1], ln_b[i, 1]
        if i % 2 == 0:
            xp = _ffn(xp, w_f_in_b, w_f_out_b, g1, b1, layer=f, tm=FFN_ROW_TILE)
            xs = _ffn(xs, w_f_in_b, w_f_out_b, g1, b1, layer=f, tm=n_s)
        else:
            xp, experts = _moe(xp, w_r[f], b_r[f], g1, b1, tm=ROW_TILE, cast=(w_e_in, w_e_out, f))
            xs, _ = _moe(xs, w_r[f], b_r[f], g1, b1, tm=n_s, experts=experts)
    return (xp.reshape(bsz, t, d), xs.reshape(dbs, s, d), jnp.stack(a_v_s), jnp.stack(b_k_p),
            jnp.stack(b_v_p), jnp.stack(b_k_s), jnp.stack(b_v_s), jnp.stack(c_p), jnp.stack(c_s))
```

```python
import functools

import jax
import jax.numpy as jnp
import numpy as np
from jax import lax
from jax.experimental import pallas as pl
from jax.experimental.pallas import tpu as pltpu

D_MODEL = 1024
DEPTH = 4
PAST_LEN = 1024
CHUNK = 64
N_MIXERS = 3
ALPHA = (2 * DEPTH) ** 0.25
LN_EPS = 1e-5
A_CHUNK = 128
A_GROUPS = 4
HEAD_DIM = 64
B_Q_HEADS = D_MODEL // HEAD_DIM
B_KV_HEADS = 4
B_GROUP = B_Q_HEADS // B_KV_HEADS
WINDOW = 128
BAND_CHUNKS = WINDOW // CHUNK
ROT_DIM = HEAD_DIM // 4
ROPE_THETA = 500000.0
CONV_WIDTH = 31
N_EXPERTS = 8
TOP_K = 2
MOE_BLOCK = 512

VMEM_LIMIT_BYTES = 56 * 1024 * 1024
ROW_TILE = 512
FFN_ROW_TILE = 1024
FF_SUB = 256
CONV_HALO = 32
GROUP_LANES = B_GROUP * HEAD_DIM
ROPE_LANES = 128

_BF16 = jnp.bfloat16
_F32 = jnp.float32


def _cparams(*sem):
    return pltpu.CompilerParams(dimension_semantics=sem, vmem_limit_bytes=VMEM_LIMIT_BYTES)


def _dot(a, b):
    return jnp.dot(a, b, preferred_element_type=_F32)


def _layer_norm(x, g, b):
    mu = jnp.mean(x, axis=-1, keepdims=True)
    xc = x - mu
    var = jnp.mean(xc * xc, axis=-1, keepdims=True)
    return xc * lax.rsqrt(var + LN_EPS) * g + b


def _sigmoid(x):
    return 1.0 / (1.0 + jnp.exp(-x))


def _full(shape):
    nd = len(shape)
    return pl.BlockSpec(shape, lambda *_: (0,) * nd)


def _gmlp_kernel(x_ref, w_in_ref, b_in_ref, g_a_ref, b_a_ref, w_s_ref, bz_ref, w_out_ref,
                 g_ref, b_ref, *out_refs, chunk, emit_v):
    o_ref = out_refs[0]
    x = x_ref[...]
    tm = x.shape[0]
    h = _dot(x.astype(_BF16), w_in_ref[...]) + b_in_ref[...]
    h = jax.nn.gelu(h)
    u = h[:, :D_MODEL]
    v = _layer_norm(h[:, D_MODEL:], g_a_ref[...], b_a_ref[...])
    if emit_v:
        out_refs[1][...] = v
    vb = v.astype(_BF16)
    ri = lax.broadcasted_iota(jnp.int32, (chunk, chunk), 0) // CHUNK
    ci = lax.broadcasted_iota(jnp.int32, (chunk, chunk), 1) // CHUNK
    gw = D_MODEL // A_GROUPS
    ws = [jnp.where(ci <= ri, w_s_ref[g], 0.0).astype(_BF16) for g in range(A_GROUPS)]
    rows = []
    for n in range(tm // chunk):
        r0 = n * chunk
        cols = [_dot(ws[g], vb[r0:r0 + chunk, g * gw:(g + 1) * gw]) for g in range(A_GROUPS)]
        rows.append(jnp.concatenate(cols, axis=1) + bz_ref[...])
    z = jnp.concatenate(rows, axis=0) if len(rows) > 1 else rows[0]
    y = _dot((u * z).astype(_BF16), w_out_ref[...])
    o_ref[...] = _layer_norm(ALPHA * x + y, g_ref[...], b_ref[...])


def _gmlp(x2, w_in, b_in, g_a, b_a, w_s, b_s, w_out, g, b, *, chunk, tm, emit_v):
    n = x2.shape[0]
    d = D_MODEL
    gw = d // A_GROUPS
    bz = jnp.repeat(b_s[:, :chunk].T, gw, axis=1)
    out_shape = [jax.ShapeDtypeStruct((n, d), _F32)]
    out_specs = [pl.BlockSpec((tm, d), lambda i: (i, 0))]
    if emit_v:
        out_shape.append(jax.ShapeDtypeStruct((n, d), _F32))
        out_specs.append(pl.BlockSpec((tm, d), lambda i: (i, 0)))
    res = pl.pallas_call(
        functools.partial(_gmlp_kernel, chunk=chunk, emit_v=emit_v),
        grid=(n // tm,),
        in_specs=[
            pl.BlockSpec((tm, d), lambda i: (i, 0)),
            _full((d, 2 * d)), _full((1, 2 * d)), _full((1, d)), _full((1, d)),
            _full((A_GROUPS, chunk, chunk)), _full((chunk, d)), _full((d, d)),
            _full((1, d)), _full((1, d)),
        ],
        out_specs=out_specs,
        out_shape=out_shape,
        compiler_params=_cparams("arbitrary"),
        name="gmlp_mixer",
    )(x2, w_in.astype(_BF16), b_in[None], g_a[None], b_a[None], w_s[:, :chunk, :chunk], bz,
      w_out.astype(_BF16), g[None], b[None])
    return res


def _rope_tables(pos):
    half = ROT_DIM // 2
    inv = ROPE_THETA ** (-np.arange(0, ROT_DIM, 2, dtype=np.float32) / ROT_DIM)
    lane = np.arange(ROPE_LANES) % HEAD_DIM
    inv_lane = jnp.asarray(inv[lane % half], _F32)[None, :]
    lo = jnp.asarray(lane < half)[None, :]
    hi = jnp.asarray((lane >= half) & (lane < ROT_DIM))[None, :]
    ang = pos.astype(_F32)[:, None] * inv_lane
    cos, sin = jnp.cos(ang), jnp.sin(ang)
    c = jnp.where(lo | hi, cos, 1.0)
    s_up = jnp.where(hi, sin, 0.0)
    s_dn = jnp.where(lo, -sin, 0.0)
    return c, s_up, s_dn


def _rotary(x, c, s_up, s_dn):
    half = ROT_DIM // 2
    slabs = []
    for i in range(x.shape[1] // ROPE_LANES):
        xs = x[:, i * ROPE_LANES:(i + 1) * ROPE_LANES]
        slabs.append(xs * c + pltpu.roll(xs, half, axis=1) * s_up
                     + pltpu.roll(xs, ROPE_LANES - half, axis=1) * s_dn)
    return jnp.concatenate(slabs, axis=1)


def _expand_kv_weights(w_qkv, b_qkv):
    nq = B_Q_HEADS * HEAD_DIM
    nk = B_KV_HEADS * HEAD_DIM
    scale = HEAD_DIM ** -0.5
    assert float(np.log2(scale)).is_integer()

    def rep(a):
        lead = a.shape[:-1]
        a = a.reshape(lead + (B_KV_HEADS, 1, HEAD_DIM))
        a = jnp.broadcast_to(a, lead + (B_KV_HEADS, B_GROUP, HEAD_DIM))
        return a.reshape(lead + (nq,))

    w = jnp.concatenate([w_qkv[:, :nq] * scale, rep(w_qkv[:, nq:nq + nk]), rep(w_qkv[:, nq + nk:])], axis=1)
    bb = jnp.concatenate([b_qkv[:nq] * scale, rep(b_qkv[nq:nq + nk]), rep(b_qkv[nq + nk:])])
    return w, bb


def _attend(q, kwin, vwin, valid, sink_col):
    nq = q.shape[0]
    lane_head = lax.broadcasted_iota(jnp.int32, (1, GROUP_LANES), 1) // HEAD_DIM
    masks = [(lane_head == g).astype(_F32) for g in range(B_GROUP)]
    qs = jnp.concatenate([q * masks[g] for g in range(B_GROUP)], axis=0).astype(_BF16)
    s = lax.dot_general(qs, kwin, (((1,), (1,)), ((), ())), preferred_element_type=_F32)
    if valid is not None:
        s = jnp.where(valid, s, -1e30)
    m = jnp.maximum(jnp.max(s, axis=-1, keepdims=True), sink_col)
    p = jnp.exp(s - m)
    p = p / (jnp.sum(p, axis=-1, keepdims=True) + jnp.exp(sink_col - m))
    o = _dot(p.astype(_BF16), vwin)
    out = o[0:nq] * masks[0]
    for g in range(1, B_GROUP):
        out = out + o[g * nq:(g + 1) * nq] * masks[g]
    return out


def _sink_cols(sink_ref, hk, nq):
    return jnp.concatenate(
        [jnp.full((nq, 1), sink_ref[hk * B_GROUP + g], _F32) for g in range(B_GROUP)], axis=0)


SCORE_LANES = 256
SOFTMAX_ROWS = 512


def _swa_prompt_kernel(sink_ref, x_ref, w_ref, bq_ref, c_ref, su_ref, sd_ref, w_out_ref, g_ref, b_ref,
                       o_ref, kc_ref, vc_ref, k_ext, v_ext, s_ref, p_ref):
    t = pl.program_id(1)
    tm = x_ref.shape[0]
    d = D_MODEL
    halo = BAND_CHUNKS * CHUNK
    nkeys = (BAND_CHUNKS + 1) * CHUNK
    n_chunks = tm // CHUNK
    grp = B_GROUP * CHUNK

    @pl.when(t == 0)
    def _():
        k_ext[0:halo, :] = jnp.zeros((halo, d), _BF16)
        v_ext[0:halo, :] = jnp.zeros((halo, d), _BF16)
        k_ext[halo + tm:, :] = jnp.zeros((k_ext.shape[0] - halo - tm, d), _BF16)

    x = x_ref[...]
    qkv = _dot(x.astype(_BF16), w_ref[...]) + bq_ref[...]
    qk = _rotary(qkv[:, :2 * d], c_ref[...], su_ref[...], sd_ref[...])
    q = qk[:, :d]
    k = qk[:, d:]
    v = qkv[:, 2 * d:]
    k_ext[halo:halo + tm, :] = k.astype(_BF16)
    v_ext[halo:halo + tm, :] = v.astype(_BF16)
    kc_ref[...] = k[tm - WINDOW:, :]
    vc_ref[...] = v[tm - WINDOW:, :]

    lane_head = lax.broadcasted_iota(jnp.int32, (1, GROUP_LANES), 1) // HEAD_DIM
    masks = [(lane_head == g).astype(_F32) for g in range(B_GROUP)]
    key_lane = lax.broadcasted_iota(jnp.int32, (grp, SCORE_LANES), 1)
    tails = [jnp.where(key_lane == nkeys, _sink_cols(sink_ref, hk, CHUNK), -1e30) for hk in range(B_KV_HEADS)]

    for n in range(n_chunks):
        r0 = n * CHUNK
        if n < BAND_CHUNKS:
            first = jnp.maximum((BAND_CHUNKS - n - t * n_chunks) * CHUNK, 0)
            keep = (lax.bitcast_convert_type(key_lane - first, jnp.uint32)
                    < lax.bitcast_convert_type(nkeys - first, jnp.uint32))
        else:
            keep = key_lane < nkeys
        for hk in range(B_KV_HEADS):
            l0 = hk * GROUP_LANES
            qn = q[r0:r0 + CHUNK, l0:l0 + GROUP_LANES]
            qs = jnp.concatenate([qn * masks[g] for g in range(B_GROUP)], axis=0).astype(_BF16)
            s = lax.dot_general(qs, k_ext[r0:r0 + SCORE_LANES, l0:l0 + GROUP_LANES],
                                (((1,), (1,)), ((), ())), preferred_element_type=_F32)
            i0 = (n * B_KV_HEADS + hk) * grp
            s_ref[i0:i0 + grp, :] = jnp.where(keep, s, tails[hk])

    for i in range(s_ref.shape[0] // SOFTMAX_ROWS):
        s = s_ref[i * SOFTMAX_ROWS:(i + 1) * SOFTMAX_ROWS, :]
        p = jnp.exp(s - jnp.max(s, axis=-1, keepdims=True))
        p_ref[i * SOFTMAX_ROWS:(i + 1) * SOFTMAX_ROWS, :] = (p / jnp.sum(p, axis=-1, keepdims=True)).astype(_BF16)

    rows = []
    for n in range(n_chunks):
        r0 = n * CHUNK
        cols = []
        for hk in range(B_KV_HEADS):
            l0 = hk * GROUP_LANES
            i0 = (n * B_KV_HEADS + hk) * grp
            o = _dot(p_ref[i0:i0 + grp, 0:nkeys], v_ext[r0:r0 + nkeys, l0:l0 + GROUP_LANES])
            out = o[0:CHUNK] * masks[0]
            for g in range(1, B_GROUP):
                out = out + o[g * CHUNK:(g + 1) * CHUNK] * masks[g]
            cols.append(out)
        rows.append(jnp.concatenate(cols, axis=1))
    att = jnp.concatenate(rows, axis=0)
    y = _dot(att.astype(_BF16), w_out_ref[...])
    o_ref[...] = _layer_norm(ALPHA * x + y, g_ref[...], b_ref[...])
    k_ext[0:halo, :] = k_ext[tm:tm + halo, :]
    v_ext[0:halo, :] = v_ext[tm:tm + halo, :]


def _unexpand(a):
    lead = a.shape[:-1]
    return a.reshape(lead + (B_KV_HEADS, B_GROUP, HEAD_DIM))[..., 0, :]


def _swa_prompt(x3, w_qkv, b_qkv, sink, w_out, g, b, *, tm):
    bsz, t, d = x3.shape
    w_exp, b_exp = _expand_kv_weights(w_qkv, b_qkv)
    c, su, sd = _rope_tables(jnp.arange(t))
    tab = pl.BlockSpec((tm, 2 * HEAD_DIM), lambda bi, ti, *_: (ti, 0))
    xspec = pl.BlockSpec((None, tm, d), lambda bi, ti, *_: (bi, ti, 0))
    cspec = pl.BlockSpec((None, WINDOW, d), lambda bi, ti, *_: (bi, 0, 0))
    halo = BAND_CHUNKS * CHUNK
    score_rows = (tm // CHUNK) * B_KV_HEADS * B_GROUP * CHUNK
    assert score_rows % SOFTMAX_ROWS == 0
    one = pl.Buffered(1)
    y, kc, vc = pl.pallas_call(
        _swa_prompt_kernel,
        grid_spec=pltpu.PrefetchScalarGridSpec(
            num_scalar_prefetch=1,
            grid=(bsz, t // tm),
            in_specs=[xspec, pl.BlockSpec((d, 3 * d), lambda *_: (0, 0), pipeline_mode=one),
                      _full((1, 3 * d)), tab, tab, tab,
                      pl.BlockSpec((d, d), lambda *_: (0, 0), pipeline_mode=one),
                      _full((1, d)), _full((1, d))],
            out_specs=[xspec, cspec, cspec],
            scratch_shapes=[
                pltpu.VMEM((halo + tm + SCORE_LANES - (halo + CHUNK), d), _BF16),
                pltpu.VMEM((halo + tm, d), _BF16),
                pltpu.VMEM((score_rows, SCORE_LANES), _F32),
                pltpu.VMEM((score_rows, SCORE_LANES), _BF16)],
        ),
        out_shape=[jax.ShapeDtypeStruct((bsz, t, d), _F32),
                   jax.ShapeDtypeStruct((bsz, WINDOW, d), _F32),
                   jax.ShapeDtypeStruct((bsz, WINDOW, d), _F32)],
        compiler_params=_cparams("arbitrary", "arbitrary"),
        name="swa_prompt_mixer",
    )(sink, x3, w_exp.astype(_BF16), b_exp[None], c, su, sd, w_out.astype(_BF16), g[None], b[None])
    return y, _unexpand(kc), _unexpand(vc)


def _swa_sample_kernel(sink_ref, x_ref, ck_ref, cv_ref, valid_ref, w_ref, bq_ref, c_ref, su_ref, sd_ref,
                       w_out_ref, g_ref, b_ref, o_ref, kn_ref, vn_ref, *, bsz, s):
    d = D_MODEL
    x = x_ref[...]
    qkv = _dot(x.astype(_BF16), w_ref[...]) + bq_ref[...]
    qk = _rotary(qkv[:, :2 * d], c_ref[...], su_ref[...], sd_ref[...])
    q = qk[:, :d]
    k = qk[:, d:]
    v = qkv[:, 2 * d:]
    kn_ref[...] = k
    vn_ref[...] = v
    kb = k.astype(_BF16)
    vb = v.astype(_BF16)
    valid = jnp.concatenate([valid_ref[...] > 0] * B_GROUP, axis=0)
    rows = []
    for bi in range(bsz):
        r0 = bi * s
        kk = jnp.concatenate([ck_ref[bi], kb[r0:r0 + s]], axis=0)
        vv = jnp.concatenate([cv_ref[bi], vb[r0:r0 + s]], axis=0)
        cols = []
        for hk in range(B_KV_HEADS):
            l0 = hk * GROUP_LANES
            cols.append(_attend(q[r0:r0 + s, l0:l0 + GROUP_LANES], kk[:, l0:l0 + GROUP_LANES],
                                vv[:, l0:l0 + GROUP_LANES], valid, _sink_cols(sink_ref, hk, s)))
        rows.append(jnp.concatenate(cols, axis=1))
    att = jnp.concatenate(rows, axis=0)
    y = _dot(att.astype(_BF16), w_out_ref[...])
    o_ref[...] = _layer_norm(ALPHA * x + y, g_ref[...], b_ref[...])


def _swa_sample(x3, ck, cv, w_qkv, b_qkv, sink, w_out, g, b):
    bsz, s, d = x3.shape
    nc = ck.shape[1]
    w_exp, b_exp = _expand_kv_weights(w_qkv, b_qkv)
    qpos = PAST_LEN + np.arange(s)
    kpos = np.concatenate([PAST_LEN - nc + np.arange(nc), qpos])
    qc, kc = qpos // CHUNK, kpos // CHUNK
    valid = ((kc[None, :] <= qc[:, None]) & (kc[None, :] >= qc[:, None] - BAND_CHUNKS)).astype(np.int32)
    c, su, sd = _rope_tables(jnp.tile(jnp.asarray(qpos), bsz))

    def expand_cache(a):
        a = jnp.broadcast_to(a[:, :, :, None, :], (bsz, nc, B_KV_HEADS, B_GROUP, HEAD_DIM))
        return a.reshape(bsz, nc, d).astype(_BF16)

    n = bsz * s
    vm = pl.BlockSpec(memory_space=pltpu.VMEM)
    y, kn, vn = pl.pallas_call(
        functools.partial(_swa_sample_kernel, bsz=bsz, s=s),
        in_specs=[pl.BlockSpec(memory_space=pltpu.SMEM)] + [vm] * 12,
        out_specs=[vm, vm, vm],
        out_shape=[jax.ShapeDtypeStruct((n, d), _F32)] * 3,
        compiler_params=pltpu.CompilerParams(vmem_limit_bytes=VMEM_LIMIT_BYTES),
        name="swa_sample_mixer",
    )(sink, x3.reshape(n, d), expand_cache(ck), expand_cache(cv), jnp.asarray(valid),
      w_exp.astype(_BF16), b_exp[None], c, su, sd, w_out.astype(_BF16), g[None], b[None])
    return (y.reshape(bsz, s, d), _unexpand(kn.reshape(bsz, s, d)), _unexpand(vn.reshape(bsz, s, d)))


def _conv_kernel(x_ref, st_ref, w_in_ref, b_in_ref, w_dw_ref, b_dw_ref, g_c_ref, b_c_ref, w_out_ref,
                 g_ref, b_ref, o_ref, st_out_ref, h_ext):
    t = pl.program_id(1)
    tm = x_ref.shape[0]
    d = D_MODEL

    @pl.when(t == 0)
    def _():
        h_ext[0:CONV_HALO, :] = st_ref[...]

    x = x_ref[...]
    ag = _dot(x.astype(_BF16), w_in_ref[...]) + b_in_ref[...]
    h = ag[:, :d] * _sigmoid(ag[:, d:])
    h_ext[CONV_HALO:CONV_HALO + tm, :] = h
    base = CONV_HALO - (CONV_WIDTH - 1)
    acc = None
    for sub in range(8):
        offs = [o for o in range(base, base + CONV_WIDTH) if o % 8 == sub]
        if not offs:
            continue
        rows = tm + (8 if sub else 0)
        part = None
        for o in offs:
            a0 = o - sub
            term = h_ext[a0:a0 + rows, :] * w_dw_ref[o - base:o - base + 1, :]
            part = term if part is None else part + term
        part = part[sub:sub + tm, :]
        acc = part if acc is None else acc + part
    y = acc + b_dw_ref[...]
    y = _layer_norm(y, g_c_ref[...], b_c_ref[...])
    y = y * _sigmoid(y)
    y = _dot(y.astype(_BF16), w_out_ref[...])
    o_ref[...] = _layer_norm(ALPHA * x + y, g_ref[...], b_ref[...])
    new_hist = h_ext[tm:tm + CONV_HALO, :]
    st_out_ref[...] = new_hist
    h_ext[0:CONV_HALO, :] = new_hist


def _conv(x3, state, w_in, b_in, w_dw, b_dw, g_c, b_c, w_out, g, b, *, tm):
    bsz, t, d = x3.shape
    st = jnp.pad(state, ((0, 0), (CONV_HALO - (CONV_WIDTH - 1), 0), (0, 0)))
    xspec = pl.BlockSpec((None, tm, d), lambda bi, ti: (bi, ti, 0))
    sspec = pl.BlockSpec((None, CONV_HALO, d), lambda bi, ti: (bi, 0, 0))
    y, st_new = pl.pallas_call(
        _conv_kernel,
        grid=(bsz, t // tm),
        in_specs=[xspec, sspec, _full((d, 2 * d)), _full((1, 2 * d)), _full((CONV_WIDTH, d)), _full((1, d)),
                  _full((1, d)), _full((1, d)), _full((d, d)), _full((1, d)), _full((1, d))],
        out_specs=[xspec, sspec],
        out_shape=[jax.ShapeDtypeStruct((bsz, t, d), _F32),
                   jax.ShapeDtypeStruct((bsz, CONV_HALO, d), _F32)],
        scratch_shapes=[pltpu.VMEM((tm + CONV_HALO, d), _F32)],
        compiler_params=_cparams("arbitrary", "arbitrary"),
        name="conv_mixer",
    )(x3, st, w_in.astype(_BF16), b_in[None], w_dw, b_dw[None], g_c[None], b_c[None],
      w_out.astype(_BF16), g[None], b[None])
    return y, st_new[:, CONV_HALO - (CONV_WIDTH - 1):]


def _swiglu_partial(xb, wg_ref, wu_ref, wo_ref, width):
    acc = None
    for c in range(width // FF_SUB):
        sl = slice(c * FF_SUB, (c + 1) * FF_SUB)
        gt = _dot(xb, wg_ref[:, sl])
        up = _dot(xb, wu_ref[:, sl])
        hh = (gt * _sigmoid(gt) * up).astype(_BF16)
        part = _dot(hh, wo_ref[sl, :])
        acc = part if acc is None else acc + part
    return acc


def _ffn_kernel(x_ref, wg_ref, wu_ref, wo_ref, g_ref, b_ref, o_ref, *, d_ff):
    x = x_ref[...]
    f = _swiglu_partial(x.astype(_BF16), wg_ref, wu_ref, wo_ref, d_ff)
    o_ref[...] = _layer_norm(ALPHA * x + f, g_ref[...], b_ref[...])


def _ffn(x2, w_in, w_out, g, b, *, layer, tm):
    n, d = x2.shape
    d_ff = w_out.shape[1]
    one = pl.Buffered(1)
    return pl.pallas_call(
        functools.partial(_ffn_kernel, d_ff=d_ff),
        grid=(n // tm,),
        in_specs=[
            pl.BlockSpec((tm, d), lambda i: (i, 0)),
            pl.BlockSpec((None, d, d_ff), lambda i: (layer, 0, 0), pipeline_mode=one),
            pl.BlockSpec((None, d, d_ff), lambda i: (layer, 0, 1), pipeline_mode=one),
            pl.BlockSpec((None, d_ff, d), lambda i: (layer, 0, 0), pipeline_mode=one),
            _full((1, d)), _full((1, d)),
        ],
        out_specs=pl.BlockSpec((tm, d), lambda i: (i, 0)),
        out_shape=jax.ShapeDtypeStruct((n, d), _F32),
        compiler_params=_cparams("arbitrary"),
        name="dense_swiglu",
    )(x2, w_in, w_in, w_out, g[None], b[None])


def _moe_kernel(blk_e_ref, nblk_ref, x_ref, wg_ref, wu_ref, wo_ref, o_ref, acc_ref, *, width):
    bi = pl.program_id(0)
    j = pl.program_id(1)
    last = pl.num_programs(1) - 1
    live = bi < nblk_ref[0]

    @pl.when(live)
    def _():
        part = _swiglu_partial(x_ref[...].astype(_BF16), wg_ref, wu_ref, wo_ref, width)

        @pl.when(j == 0)
        def _():
            acc_ref[...] = part

        @pl.when((j > 0) & (j < last))
        def _():
            acc_ref[...] += part

        @pl.when(j == last)
        def _():
            o_ref[...] = acc_ref[...] + part

    @pl.when(jnp.logical_not(live) & (j == last))
    def _():
        o_ref[...] = jnp.zeros(o_ref.shape, o_ref.dtype)


def _moe_experts(xbuf, blk_e, nblk, w_e_in, w_e_out, *, blk):
    rows, d = xbuf.shape
    n_blocks = rows // blk
    d_ff = w_e_out.shape[1]
    splits = 2
    width = d_ff // splits

    def wj(i, j, nb):
        return jnp.where(i < nb[0], j, splits - 1)

    return pl.pallas_call(
        functools.partial(_moe_kernel, width=width),
        grid_spec=pltpu.PrefetchScalarGridSpec(
            num_scalar_prefetch=2,
            grid=(n_blocks, splits),
            in_specs=[
                pl.BlockSpec((blk, d), lambda i, j, e, nb: (i, 0)),
                pl.BlockSpec((None, d, width), lambda i, j, e, nb: (e[i], 0, wj(i, j, nb))),
                pl.BlockSpec((None, d, width), lambda i, j, e, nb: (e[i], 0, splits + wj(i, j, nb))),
                pl.BlockSpec((None, width, d), lambda i, j, e, nb: (e[i], wj(i, j, nb), 0)),
            ],
            out_specs=pl.BlockSpec((blk, d), lambda i, j, e, nb: (i, 0)),
            scratch_shapes=[pltpu.VMEM((blk, d), _F32)],
        ),
        out_shape=jax.ShapeDtypeStruct((rows, d), _F32),
        compiler_params=_cparams("arbitrary", "arbitrary"),
        name="moe_experts",
    )(blk_e, nblk, xbuf, w_e_in, w_e_in, w_e_out)


ROUTE_LANES = 128


def _router_kernel(x_ref, wh_ref, wl_ref, br_ref, *rest, cast_weights):
    if cast_weights:
        w_ref, info_ref, cnt_ref, w_b_ref, base_ref = rest
        w_b_ref[...] = w_ref[...].astype(_BF16)
    else:
        info_ref, cnt_ref, base_ref = rest

    @pl.when(pl.program_id(0) == 0)
    def _():
        base_ref[...] = jnp.zeros(base_ref.shape, _F32)

    x = x_ref[...]
    tm = x.shape[0]
    xh = x.astype(_BF16)
    xl = (x - xh.astype(_F32)).astype(_BF16)
    logits = _dot(xh, wh_ref[...]) + (_dot(xh, wl_ref[...]) + _dot(xl, wh_ref[...])) + br_ref[...]
    lane = lax.broadcasted_iota(jnp.int32, logits.shape, 1)
    neg = -jnp.inf
    logits = jnp.where(lane < N_EXPERTS, logits, neg)
    m1 = jnp.max(logits, axis=-1, keepdims=True)
    i1 = jnp.min(jnp.where(logits == m1, lane, ROUTE_LANES), axis=-1, keepdims=True)
    rest = jnp.where(lane == i1, neg, logits)
    m2 = jnp.max(rest, axis=-1, keepdims=True)
    i2 = jnp.min(jnp.where(rest == m2, lane, ROUTE_LANES), axis=-1, keepdims=True)
    ex = jnp.exp(m2 - m1)
    g1 = 1.0 / (1.0 + ex)
    g2 = ex / (1.0 + ex)
    oh1 = (lane == i1).astype(_F32)
    oh2 = (lane == i2).astype(_F32)
    oh = oh1 + oh2
    tri = (lax.broadcasted_iota(jnp.int32, (tm, tm), 0) > lax.broadcasted_iota(jnp.int32, (tm, tm), 1))
    before = _dot(tri.astype(_BF16), oh.astype(_BF16)) + base_ref[...]
    r1 = jnp.sum(before * oh1, axis=-1, keepdims=True)
    r2 = jnp.sum(before * oh2, axis=-1, keepdims=True)
    base_ref[...] += jnp.sum(oh, axis=0, keepdims=True)
    cnt_ref[...] = base_ref[...]
    cols = (i1.astype(_F32), i2.astype(_F32), r1, r2, g1, g2)
    info = jnp.zeros(logits.shape, _F32)
    for c, val in enumerate(cols):
        info = jnp.where(lane == c, val, info)
    info_ref[...] = info


def _cast_specs(w, layer, n_tiles):
    n_e = w.shape[1]
    per_e = n_tiles // n_e
    rows, cols = w.shape[2] // per_e, w.shape[3]
    assert n_tiles == per_e * n_e and w.shape[2] == rows * per_e and rows % 16 == 0
    w_spec = pl.BlockSpec((None, None, rows, cols), lambda i, *_: (layer, i // per_e, i % per_e, 0))
    wb_spec = pl.BlockSpec((None, rows, cols), lambda i, *_: (i // per_e, i % per_e, 0))
    return w_spec, wb_spec, jax.ShapeDtypeStruct(w.shape[1:], _BF16)


def _router(x2, w_r, b_r, *, tm, cast=None):
    n, d = x2.shape
    n_tiles = n // tm
    wp = jnp.pad(w_r, ((0, 0), (0, ROUTE_LANES - N_EXPERTS)))
    wh = wp.astype(_BF16)
    wl = (wp - wh.astype(_F32)).astype(_BF16)
    bp = jnp.pad(b_r, (0, ROUTE_LANES - N_EXPERTS))[None]
    in_specs = [pl.BlockSpec((tm, d), lambda i: (i, 0)), _full((d, ROUTE_LANES)), _full((d, ROUTE_LANES)),
                _full((1, ROUTE_LANES))]
    out_specs = [pl.BlockSpec((tm, ROUTE_LANES), lambda i: (i, 0)), _full((1, ROUTE_LANES))]
    out_shape = [jax.ShapeDtypeStruct((n, ROUTE_LANES), _F32), jax.ShapeDtypeStruct((1, ROUTE_LANES), _F32)]
    args = [x2, wh, wl, bp]
    if cast is not None:
        w_spec, wb_spec, wb_shape = _cast_specs(*cast, n_tiles)
        in_specs.append(w_spec)
        out_specs.append(wb_spec)
        out_shape.append(wb_shape)
        args.append(cast[0])
    return pl.pallas_call(
        functools.partial(_router_kernel, cast_weights=cast is not None),
        grid=(n_tiles,),
        in_specs=in_specs,
        out_specs=out_specs,
        out_shape=out_shape,
        scratch_shapes=[pltpu.VMEM((1, ROUTE_LANES), _F32)],
        compiler_params=_cparams("arbitrary"),
        name="moe_router",
    )(*args)


def _route_plan(info, cnt, *, blk, n_blocks):
    e = info[:, 0:TOP_K].astype(jnp.int32)
    rank = info[:, TOP_K:2 * TOP_K].astype(jnp.int32)
    counts = cnt[0, :N_EXPERTS].astype(jnp.int32)
    padded = (counts + blk - 1) // blk * blk
    pad_end = jnp.cumsum(padded)
    pad_start = pad_end - padded
    dest = (pad_start[e] + rank).reshape(-1)
    nblk = pad_end[-1] // blk
    def owner(bounds, q):
        return jnp.minimum(jnp.sum(bounds[None, :] <= q[:, None], axis=1), N_EXPERTS - 1)

    blk_e = owner(pad_end, jnp.minimum(jnp.arange(n_blocks), nblk - 1) * blk).astype(jnp.int32)
    n_fill = n_blocks * blk - dest.shape[0]
    tail = padded - counts
    gaps = jnp.cumsum(tail)
    slot = jnp.arange(n_fill)
    ge = owner(gaps, slot)
    in_gap = pad_start[ge] + counts[ge] + slot - (gaps[ge] - tail[ge])
    fill = jnp.where(slot < gaps[-1], in_gap, pad_end[-1] + slot - gaps[-1]).astype(jnp.int32)
    return dest.astype(jnp.int32), blk_e, nblk.astype(jnp.int32)[None], fill


def _row_copy(src_ref, src_row, dst_ref, dst_row, sem):
    return pltpu.make_async_copy(src_ref.at[pl.ds(src_row, 1)], dst_ref.at[pl.ds(dst_row, 1)], sem)


def _dispatch_kernel(fill_ref, dest_ref, x_ref, *rest, n_fill, cast_weights):
    if cast_weights:
        w_ref, xbuf_ref, w_b_ref, zero_ref, sem, zsem = rest
        w_b_ref[...] = w_ref[...].astype(_BF16)
    else:
        xbuf_ref, zero_ref, sem, zsem = rest
    tm = x_ref.shape[0]
    zrows = zero_ref.shape[0]

    def issue(r, c):
        for k in range(TOP_K):
            _row_copy(x_ref, r, xbuf_ref, dest_ref[TOP_K * r + k], sem).start(priority=k % 2)
        return c

    lax.fori_loop(0, tm, issue, 0, unroll=8)

    @pl.when(pl.program_id(0) == pl.num_programs(0) - 1)
    def _():
        zero_ref[...] = jnp.zeros(zero_ref.shape, zero_ref.dtype)

        def zissue(r, c):
            _row_copy(zero_ref, 0, xbuf_ref, fill_ref[r], zsem).start()
            return c

        lax.fori_loop(0, n_fill, zissue, 0, unroll=8)

        def zwait(r, c):
            pltpu.make_async_copy(zero_ref, xbuf_ref.at[pl.ds(0, zrows)], zsem).wait()
            return c

        lax.fori_loop(0, n_fill // zrows, zwait, 0)

    for k in range(TOP_K):
        pltpu.make_async_copy(x_ref, xbuf_ref.at[pl.ds(0, tm)], sem).wait()


def _dispatch(x2, dest, fill, *, rows, tm, cast=None):
    n, d = x2.shape
    n_tiles = n // tm
    n_fill = fill.shape[0]
    zrows = 8
    assert n_fill % zrows == 0
    in_specs = [pl.BlockSpec((TOP_K * tm,), lambda i, f: (i,), memory_space=pltpu.SMEM),
                pl.BlockSpec((tm, d), lambda i, f: (i, 0))]
    out_specs = [pl.BlockSpec(memory_space=pl.ANY)]
    out_shape = [jax.ShapeDtypeStruct((rows, d), _F32)]
    args = [fill, dest, x2]
    if cast is not None:
        w_spec, wb_spec, wb_shape = _cast_specs(*cast, n_tiles)
        in_specs.append(w_spec)
        out_specs.append(wb_spec)
        out_shape.append(wb_shape)
        args.append(cast[0])
    return pl.pallas_call(
        functools.partial(_dispatch_kernel, n_fill=n_fill, cast_weights=cast is not None),
        grid_spec=pltpu.PrefetchScalarGridSpec(
            num_scalar_prefetch=1,
            grid=(n_tiles,),
            in_specs=in_specs,
            out_specs=out_specs,
            scratch_shapes=[pltpu.VMEM((zrows, d), _F32), pltpu.SemaphoreType.DMA, pltpu.SemaphoreType.DMA],
        ),
        out_shape=out_shape,
        compiler_params=_cparams("arbitrary"),
        name="moe_dispatch",
    )(*args)


def _combine_kernel(dest_ref, next_ref, x_ref, info_ref, ybuf_ref, g_ref, b_ref, o_ref, y_ref, sems):
    i = pl.program_id(0)
    tm = x_ref.shape[0]
    slot = i % 2
    nslot = 1 - slot

    def gather_tile(idx_ref, to_slot):
        def issue(r, c):
            for k in range(TOP_K):
                _row_copy(ybuf_ref, idx_ref[TOP_K * r + k], y_ref.at[to_slot, k], r,
                          sems.at[to_slot, k]).start(priority=k % 2)
            return c

        lax.fori_loop(0, tm, issue, 0, unroll=8)

    def wait_tile(in_slot):
        for k in range(TOP_K):
            pltpu.make_async_copy(ybuf_ref.at[pl.ds(0, tm)], y_ref.at[in_slot, k], sems.at[in_slot, k]).wait()

    @pl.when(i == 0)
    def _():
        gather_tile(dest_ref, slot)

    gather_tile(next_ref, nslot)
    wait_tile(slot)
    info = info_ref[...]
    f = y_ref[slot, 0] * info[:, 2 * TOP_K:2 * TOP_K + 1]
    for k in range(1, TOP_K):
        f = f + y_ref[slot, k] * info[:, 2 * TOP_K + k:2 * TOP_K + k + 1]
    o_ref[...] = _layer_norm(ALPHA * x_ref[...] + f, g_ref[...], b_ref[...])

    @pl.when(i == pl.num_programs(0) - 1)
    def _():
        wait_tile(nslot)


def _combine(x2, info, dest, ybuf, g, b, *, tm):
    n, d = x2.shape
    n_tiles = n // tm
    row = pl.BlockSpec((tm, d), lambda i: (i, 0))
    idx = lambda f: pl.BlockSpec((TOP_K * tm,), f, memory_space=pltpu.SMEM)
    return pl.pallas_call(
        _combine_kernel,
        grid=(n_tiles,),
        in_specs=[idx(lambda i: (i,)), idx(lambda i: (jnp.minimum(i + 1, n_tiles - 1),)), row,
                  pl.BlockSpec((tm, ROUTE_LANES), lambda i: (i, 0)), pl.BlockSpec(memory_space=pl.ANY),
                  _full((1, d)), _full((1, d))],
        out_specs=row,
        out_shape=jax.ShapeDtypeStruct((n, d), _F32),
        scratch_shapes=[pltpu.VMEM((2, TOP_K, tm, d), _F32), pltpu.SemaphoreType.DMA((2, TOP_K))],
        compiler_params=_cparams("arbitrary"),
        name="moe_combine",
    )(dest, dest, x2, info, ybuf, g[None], b[None])


def _moe(x2, w_r, b_r, g, b, *, tm, cast=None, experts=None):
    n, d = x2.shape
    n_assign = n * TOP_K
    blk = min(MOE_BLOCK, max(8, -(-2 * n_assign // (8 * N_EXPERTS)) * 8))
    n_blocks = -(-(n_assign + N_EXPERTS * (blk - 1)) // blk)
    if experts is None:
        w_e_in, w_e_out, layer = cast
        info, cnt, w_in_b = _router(x2, w_r, b_r, tm=tm, cast=(w_e_in, layer))
    else:
        info, cnt = _router(x2, w_r, b_r, tm=tm)
    dest, blk_e, nblk, fill = _route_plan(info, cnt, blk=blk, n_blocks=n_blocks)
    if experts is None:
        xbuf, w_out_b = _dispatch(x2, dest, fill, rows=n_blocks * blk, tm=tm, cast=(w_e_out, layer))
        experts = (w_in_b, w_out_b)
    else:
        (xbuf,) = _dispatch(x2, dest, fill, rows=n_blocks * blk, tm=tm)
    ybuf = _moe_experts(xbuf, blk_e, nblk, *experts, blk=blk)
    return _combine(x2, info, dest, ybuf, g, b, tm=tm), tuple(experts)


def kernel(x_prompt, x_sample, cache_b_k, cache_b_v, state_c_conv, ln_g, ln_b, w_a_in, b_a_in, ln_a_g, ln_a_b,
           w_a_s, b_a_s, w_a_out, w_b_qkv, b_b_qkv, b_sink, w_b_out, w_c_in, b_c_in, w_c_dw, b_c_dw, ln_c_g,
           ln_c_b, w_c_out, w_f_in, w_f_out, w_r, b_r, w_e_in, w_e_out):
    bsz, t, d = x_prompt.shape
    dbs, s, _ = x_sample.shape
    n_p, n_s = bsz * t, dbs * s
    xp = x_prompt.reshape(n_p, d)
    xs = x_sample.reshape(n_s, d)
    a_v_s, b_k_p, b_v_p, b_k_s, b_v_s, c_p, c_s = [], [], [], [], [], [], []
    w_f_in_b, w_f_out_b = w_f_in.astype(_BF16), w_f_out.astype(_BF16)
    for i in range(DEPTH):
        kind, j = i % N_MIXERS, i // N_MIXERS
        g0, b0 = ln_g[i, 0], ln_b[i, 0]
        if kind == 0:
            args = (w_a_in[j], b_a_in[j], ln_a_g[j], ln_a_b[j], w_a_s[j], b_a_s[j], w_a_out[j], g0, b0)
            (xp,) = _gmlp(xp, *args, chunk=A_CHUNK, tm=ROW_TILE, emit_v=False)
            xs, v_rows = _gmlp(xs, *args, chunk=s, tm=n_s, emit_v=True)
            a_v_s.append(v_rows.reshape(dbs, s, d))
        elif kind == 1:
            args = (w_b_qkv[j], b_b_qkv[j], b_sink[j], w_b_out[j], g0, b0)
            xp3, kp, vp = _swa_prompt(xp.reshape(bsz, t, d), *args, tm=ROW_TILE)
            xs3, kn, vn = _swa_sample(xs.reshape(dbs, s, d), cache_b_k[j], cache_b_v[j], *args)
            xp, xs = xp3.reshape(n_p, d), xs3.reshape(n_s, d)
            b_k_p.append(kp)
            b_v_p.append(vp)
            b_k_s.append(kn)
            b_v_s.append(vn)
        else:
            args = (w_c_in[j], b_c_in[j], w_c_dw[j], b_c_dw[j], ln_c_g[j], ln_c_b[j], w_c_out[j], g0, b0)
            zero_state = jnp.zeros((bsz, CONV_WIDTH - 1, d), _F32)
            xp3, cp = _conv(xp.reshape(bsz, t, d), zero_state, *args, tm=ROW_TILE)
            xs3, cs = _conv(xs.reshape(dbs, s, d), state_c_conv[j], *args, tm=s)
            xp, xs = xp3.reshape(n_p, d), xs3.reshape(n_s, d)
            c_p.append(cp)
            c_s.append(cs)
        f = i // 2
        g1, b1 = ln_g[i, 1], ln_b[i, 1]
        if i % 2 == 0:
            xp = _ffn(xp, w_f_in_b, w_f_out_b, g1, b1, layer=f, tm=FFN_ROW_TILE)
            xs = _ffn(xs, w_f_in_b, w_f_out_b, g1, b1, layer=f, tm=n_s)
        else:
            xp, experts = _moe(xp, w_r[f], b_r[f], g1, b1, tm=ROW_TILE, cast=(w_e_in, w_e_out, f))
            xs, _ = _moe(xs, w_r[f], b_r[f], g1, b1, tm=n_s, experts=experts)
    return (xp.reshape(bsz, t, d), xs.reshape(dbs, s, d), jnp.stack(a_v_s), jnp.stack(b_k_p),
            jnp.stack(b_v_p), jnp.stack(b_k_s), jnp.stack(b_v_s), jnp.stack(c_p), jnp.stack(c_s))
```
